```python
import jax, jax.numpy as jnp
from jax import lax
import numpy as np

D_MODEL = 2048
BATCH = 16
SEQ = 2048
DEPTH = 2
DEC_BATCH = 16
DEC_SEQ = 64
PAST_LEN = 4096

CHUNK = 64
Q_BLOCK = 128
SB_HEADS = D_MODEL // 256
SB_HEAD_DIM = 128
SB_WIDTH = SB_HEADS * SB_HEAD_DIM
MLA_HEADS = D_MODEL // 256
MLA_NOPE_DIM = 128
MLA_ROPE_DIM = 64
MLA_QK_DIM = MLA_NOPE_DIM + MLA_ROPE_DIM
MLA_V_DIM = 128
MLA_LATENT = D_MODEL // 4
MLA_Q_WIDTH = MLA_HEADS * MLA_QK_DIM
MLA_V_WIDTH = MLA_HEADS * MLA_V_DIM
GATE_WIDTH = 2 * D_MODEL
_END_SB_Q = SB_WIDTH
_END_SB_K = 2 * SB_WIDTH
_END_SB_V = 3 * SB_WIDTH
_END_MLA_Q = _END_SB_V + MLA_Q_WIDTH
_END_CKV = _END_MLA_Q + MLA_LATENT
_END_KROPE = _END_CKV + MLA_ROPE_DIM
IN_SPLITS = (_END_SB_Q, _END_SB_K, _END_SB_V, _END_MLA_Q, _END_CKV, _END_KROPE)
IN_WIDTH = _END_KROPE + GATE_WIDTH
D_FF = 4 * D_MODEL
ROPE_THETA = 10000.0
NORM_EPS = 1e-6
NEG_INF = -1e30

kernel_name = 'stickbreak_mla_gated_hybrid_stream_step'


def rms_norm(x, g):
    xf = x.astype(jnp.float32)
    y = xf * lax.rsqrt(jnp.mean(xf * xf, axis=-1, keepdims=True) + NORM_EPS)
    return (y * g.astype(jnp.float32)).astype(x.dtype)


def apply_rope(x, pos):
    half = x.shape[-1] // 2
    inv_freq = ROPE_THETA ** (-jnp.arange(half, dtype=jnp.float32) / half)
    ang = pos.astype(jnp.float32)[:, None] * inv_freq[None, :]
    cos = jnp.cos(ang)[None, :, None, :]
    sin = jnp.sin(ang)[None, :, None, :]
    xf = x.astype(jnp.float32)
    x1, x2 = xf[..., :half], xf[..., half:]
    return jnp.concatenate([x1 * cos - x2 * sin, x1 * sin + x2 * cos], axis=-1).astype(x.dtype)


def stick_breaking(q, k, v, qpos, kpos):
    scale = q.shape[-1] ** -0.5
    z = jnp.einsum('bqhd,bkhd->bhqk', q, k).astype(jnp.float32) * scale
    mask = kpos[None, :] < qpos[:, None]
    log_1m = jnp.where(mask, jax.nn.log_sigmoid(-z), 0.0)
    log_rest = lax.cumsum(log_1m, axis=3, reverse=True) - log_1m
    a = jnp.where(mask, jnp.exp(jax.nn.log_sigmoid(z) + log_rest), 0.0)
    return jnp.einsum('bhqk,bkhd->bqhd', a.astype(v.dtype), v)


def mla_attend(q, k, v, qpos, kpos):
    scale = MLA_QK_DIM ** -0.5
    s = jnp.einsum('bqhd,bkhd->bhqk', q, k).astype(jnp.float32) * scale
    mask = (kpos // CHUNK)[None, :] <= (qpos // CHUNK)[:, None]
    p = jax.nn.softmax(jnp.where(mask, s, NEG_INF), axis=-1)
    return jnp.einsum('bhqk,bkhd->bqhd', p.astype(v.dtype), v)


def sweep_query_blocks(attend, q):
    B, S, H, D = q.shape
    nb = S // Q_BLOCK
    qb = jnp.moveaxis(q.reshape(B, nb, Q_BLOCK, H, D), 1, 0)
    starts = jnp.arange(nb, dtype=jnp.int32) * Q_BLOCK
    offs = jnp.arange(Q_BLOCK, dtype=jnp.int32)
    out = lax.map(lambda a: attend(a[0], a[1] + offs), (qb, starts))
    return jnp.moveaxis(out, 0, 1).reshape(B, S, H, out.shape[-1])


def mixer_projections(x, pos, norm1_g, w_in, q_norm_g, kv_norm_g):
    B, S, _ = x.shape
    h = rms_norm(x, norm1_g)
    z = jnp.einsum('bsd,dn->bsn', h, w_in)
    sb_q, sb_k, sb_v, q_mla, c_kv, k_rope, gate = jnp.split(z, IN_SPLITS, axis=-1)
    q_mla = q_mla.reshape(B, S, MLA_HEADS, MLA_QK_DIM)
    q_mla = jnp.concatenate([q_mla[..., :MLA_NOPE_DIM], apply_rope(q_mla[..., MLA_NOPE_DIM:], pos)], axis=-1)
    q_mla = rms_norm(q_mla, q_norm_g)
    c_kv = rms_norm(c_kv, kv_norm_g)
    k_rope = apply_rope(k_rope[:, :, None, :], pos)[:, :, 0, :]
    sb_q = sb_q.reshape(B, S, SB_HEADS, SB_HEAD_DIM)
    sb_k = sb_k.reshape(B, S, SB_HEADS, SB_HEAD_DIM)
    sb_v = sb_v.reshape(B, S, SB_HEADS, SB_HEAD_DIM)
    return sb_q, sb_k, sb_v, q_mla, c_kv, k_rope, jax.nn.sigmoid(gate)


def mla_keys_values(c_kv, k_rope, w_uk, w_uv, k_norm_g):
    B, K, _ = c_kv.shape
    k_nope = jnp.einsum('bkl,ln->bkn', c_kv, w_uk).reshape(B, K, MLA_HEADS, MLA_NOPE_DIM)
    v = jnp.einsum('bkl,ln->bkn', c_kv, w_uv).reshape(B, K, MLA_HEADS, MLA_V_DIM)
    kr = jnp.broadcast_to(k_rope[:, :, None, :], (B, K, MLA_HEADS, MLA_ROPE_DIM))
    k = rms_norm(jnp.concatenate([k_nope, kr], axis=-1), k_norm_g)
    return k, v


def merge_and_ffn(x, sb_out, mla_out, gate, w_sb_proj, w_mla_proj, w_o, norm2_g, w_up, w_down):
    B, S, _ = x.shape
    y_sb = sb_out.reshape(B, S, SB_WIDTH) @ w_sb_proj
    y_mla = mla_out.reshape(B, S, MLA_V_WIDTH) @ w_mla_proj
    g_sb, g_mla = jnp.split(gate, 2, axis=-1)
    x = x + (g_sb * y_sb + g_mla * y_mla) @ w_o
    h = rms_norm(x, norm2_g)
    return x + jnp.square(jax.nn.relu(h @ w_up)) @ w_down


def prompt_layer(x, norm1_g, w_in, q_norm_g, k_norm_g, kv_norm_g, w_uk, w_uv,
                 w_sb_proj, w_mla_proj, w_o, norm2_g, w_up, w_down):
    S = x.shape[1]
    pos = jnp.arange(S, dtype=jnp.int32)
    sb_q, sb_k, sb_v, q, c_kv, k_rope, gate = mixer_projections(x, pos, norm1_g, w_in, q_norm_g, kv_norm_g)
    sb_out = sweep_query_blocks(lambda qb, qp: stick_breaking(qb, sb_k, sb_v, qp, pos), sb_q)
    k, v = mla_keys_values(c_kv, k_rope, w_uk, w_uv, k_norm_g)
    mla_out = sweep_query_blocks(lambda qb, qp: mla_attend(qb, k, v, qp, pos), q)
    y = merge_and_ffn(x, sb_out, mla_out, gate, w_sb_proj, w_mla_proj, w_o, norm2_g, w_up, w_down)
    return y, sb_k, sb_v, c_kv, k_rope


def sample_layer(x, past_sb_k, past_sb_v, past_ckv, past_kr, norm1_g, w_in, q_norm_g, k_norm_g,
                 kv_norm_g, w_uk, w_uv, w_sb_proj, w_mla_proj, w_o, norm2_g, w_up, w_down):
    n = x.shape[1]
    past = past_sb_k.shape[1]
    pos = past + jnp.arange(n, dtype=jnp.int32)
    kpos = jnp.arange(past + n, dtype=jnp.int32)
    sb_q, sb_k, sb_v, q, c_kv, k_rope, gate = mixer_projections(x, pos, norm1_g, w_in, q_norm_g, kv_norm_g)
    k_all = jnp.concatenate([past_sb_k.astype(sb_k.dtype), sb_k], axis=1)
    v_all = jnp.concatenate([past_sb_v.astype(sb_v.dtype), sb_v], axis=1)
    sb_out = stick_breaking(sb_q, k_all, v_all, pos, kpos)
    ckv_all = jnp.concatenate([past_ckv.astype(c_kv.dtype), c_kv], axis=1)
    kr_all = jnp.concatenate([past_kr.astype(k_rope.dtype), k_rope], axis=1)
    k, v = mla_keys_values(ckv_all, kr_all, w_uk, w_uv, k_norm_g)
    mla_out = mla_attend(q, k, v, pos, kpos)
    y = merge_and_ffn(x, sb_out, mla_out, gate, w_sb_proj, w_mla_proj, w_o, norm2_g, w_up, w_down)
    return y, sb_k, sb_v, c_kv, k_rope


def setup_inputs(seed: int = 0) -> dict:
    key = jax.random.key(seed)
    ks = jax.random.split(key, 20)
    f32 = jnp.float32
    nrm = lambda k, shape, scale: jax.random.normal(k, shape, f32) * scale
    gain = lambda k, n: 1.0 + 0.02 * jax.random.normal(k, (DEPTH, n), f32)
    return {
        'x_prompt': nrm(ks[0], (BATCH, SEQ, D_MODEL), 1.0),
        'x_sample': nrm(ks[1], (DEC_BATCH, DEC_SEQ, D_MODEL), 1.0),
        'cache_sb_k': nrm(ks[2], (DEPTH, DEC_BATCH, PAST_LEN, SB_HEADS, SB_HEAD_DIM), 1.0),
        'cache_sb_v': nrm(ks[3], (DEPTH, DEC_BATCH, PAST_LEN, SB_HEADS, SB_HEAD_DIM), 1.0),
        'cache_mla_ckv': nrm(ks[4], (DEPTH, DEC_BATCH, PAST_LEN, MLA_LATENT), 1.0),
        'cache_mla_krope': nrm(ks[5], (DEPTH, DEC_BATCH, PAST_LEN, MLA_ROPE_DIM), 1.0),
        'norm1_g': gain(ks[6], D_MODEL),
        'w_in': nrm(ks[7], (DEPTH, D_MODEL, IN_WIDTH), D_MODEL ** -0.5),
        'q_norm_g': gain(ks[8], MLA_QK_DIM),
        'k_norm_g': gain(ks[9], MLA_QK_DIM),
        'kv_norm_g': gain(ks[10], MLA_LATENT),
        'w_uk': nrm(ks[11], (DEPTH, MLA_LATENT, MLA_HEADS * MLA_NOPE_DIM), MLA_LATENT ** -0.5),
        'w_uv': nrm(ks[12], (DEPTH, MLA_LATENT, MLA_V_WIDTH), MLA_LATENT ** -0.5),
        'w_sb_proj': nrm(ks[13], (DEPTH, SB_WIDTH, D_MODEL), SB_WIDTH ** -0.5),
        'w_mla_proj': nrm(ks[14], (DEPTH, MLA_V_WIDTH, D_MODEL), MLA_V_WIDTH ** -0.5),
        'w_o': nrm(ks[15], (DEPTH, D_MODEL, D_MODEL), D_MODEL ** -0.5),
        'norm2_g': gain(ks[16], D_MODEL),
        'w_up': nrm(ks[17], (DEPTH, D_MODEL, D_FF), D_MODEL ** -0.5),
        'w_down': nrm(ks[18], (DEPTH, D_FF, D_MODEL), D_FF ** -0.5),
    }


def reference(x_prompt, x_sample, cache_sb_k, cache_sb_v, cache_mla_ckv, cache_mla_krope,
              norm1_g, w_in, q_norm_g, k_norm_g, kv_norm_g, w_uk, w_uv, w_sb_proj, w_mla_proj,
              w_o, norm2_g, w_up, w_down):
    yp, ys = x_prompt, x_sample
    p_sbk, p_sbv, p_ckv, p_kr = [], [], [], []
    s_sbk, s_sbv, s_ckv, s_kr = [], [], [], []
    for l in range(DEPTH):
        lp = (norm1_g[l], w_in[l], q_norm_g[l], k_norm_g[l], kv_norm_g[l], w_uk[l], w_uv[l],
              w_sb_proj[l], w_mla_proj[l], w_o[l], norm2_g[l], w_up[l], w_down[l])
        yp, a, b, c, d = prompt_layer(yp, *lp)
        p_sbk.append(a); p_sbv.append(b); p_ckv.append(c); p_kr.append(d)
        ys, a, b, c, d = sample_layer(ys, cache_sb_k[l], cache_sb_v[l], cache_mla_ckv[l],
                                      cache_mla_krope[l], *lp)
        s_sbk.append(a); s_sbv.append(b); s_ckv.append(c); s_kr.append(d)
    return (yp, ys, jnp.stack(p_sbk), jnp.stack(p_sbv), jnp.stack(p_ckv), jnp.stack(p_kr),
            jnp.stack(s_sbk), jnp.stack(s_sbv), jnp.stack(s_ckv), jnp.stack(s_kr))
```

```python
import functools

import jax
import jax.numpy as jnp
from jax import lax
from jax.experimental import pallas as pl
from jax.experimental.pallas import tpu as pltpu

F32 = jnp.float32
BF16 = jnp.bfloat16

CHUNK = 64
HEADS = 8
HEAD_DIM = 128
ROPE_DIM = 64
HALF_ROPE = ROPE_DIM // 2
QK_DIM = HEAD_DIM + ROPE_DIM
QK_PAD = 2 * HEAD_DIM
ROPE_THETA = 10000.0
NORM_EPS = 1e-6
NEG_INF = -1e30
SB_SCALE = HEAD_DIM ** -0.5
MLA_SCALE = QK_DIM ** -0.5

VMEM_LIMIT_BYTES = 56 * 1024 * 1024


def _params(*sem):
    return pltpu.CompilerParams(dimension_semantics=sem, vmem_limit_bytes=VMEM_LIMIT_BYTES)


def _tile(m, pref):
    t = min(m, pref)
    assert m % t == 0, (m, pref)
    return t


def _nt_dot(a, b):
    return lax.dot_general(a, b, (((1,), (1,)), ((), ())), preferred_element_type=F32)


def _dot(a, b):
    return jnp.dot(a, b, preferred_element_type=F32)


def _rms(x, g):
    ms = jnp.mean(x * x, axis=-1, keepdims=True)
    return x * lax.rsqrt(ms + NORM_EPS) * g


def _rmsnorm_kernel(x_ref, g_ref, o_ref):
    o_ref[...] = _rms(x_ref[...], g_ref[...]).astype(o_ref.dtype)


def _rmsnorm(x, g, tm=512):
    m, d = x.shape
    tm = _tile(m, tm)
    return pl.pallas_call(
        _rmsnorm_kernel,
        grid=(m // tm,),
        in_specs=[pl.BlockSpec((tm, d), lambda i: (i, 0)), pl.BlockSpec((1, d), lambda i: (0, 0))],
        out_specs=pl.BlockSpec((tm, d), lambda i: (i, 0)),
        out_shape=jax.ShapeDtypeStruct((m, d), BF16),
        compiler_params=_params("parallel"),
        name="rmsnorm",
    )(x, g.reshape(1, d))


def _matmul_call(body, a, w, extras, outs, tm, tn, name):
    m, k = a.shape
    n = w.shape[1]
    tm, tn = _tile(m, tm), _tile(n, tn)
    in_specs = [pl.BlockSpec((tm, k), lambda i, j: (i, 0)), pl.BlockSpec((k, tn), lambda i, j: (0, j))]
    in_specs += [pl.BlockSpec(bs, im) for _, bs, im in extras]
    return pl.pallas_call(
        body,
        grid=(m // tm, n // tn),
        in_specs=in_specs,
        out_specs=[pl.BlockSpec(bs, im) for _, _, bs, im in outs],
        out_shape=[jax.ShapeDtypeStruct(s, d) for s, d, _, _ in outs],
        compiler_params=_params("parallel", "arbitrary"),
        name=name,
    )(a, w, *[e[0] for e in extras])


def _plain_kernel(a_ref, w_ref, *o_refs):
    acc = _dot(a_ref[...], w_ref[...])
    for o_ref in o_refs:
        o_ref[...] = acc.astype(o_ref.dtype)


def _gate_kernel(a_ref, w_ref, o_ref):
    o_ref[...] = jax.nn.sigmoid(_dot(a_ref[...], w_ref[...])).astype(o_ref.dtype)


def _rope_tile(x, tab):
    t = x * tab
    return t + pltpu.roll(t, ROPE_DIM, 1)


def _qmla_kernel(a_ref, w_ref, tab_ref, g_ref, o_ref):
    acc = _dot(a_ref[...], w_ref[...])
    tab = tab_ref[...]
    low = (lax.broadcasted_iota(jnp.int32, (1, HEAD_DIM), 1) < ROPE_DIM).astype(F32)
    g_nope, g_rope = g_ref[:, :HEAD_DIM], g_ref[:, HEAD_DIM:]
    for h in range(HEADS):
        c0 = h * QK_PAD
        nope = acc[:, c0:c0 + HEAD_DIM]
        rope = _rope_tile(acc[:, c0 + HEAD_DIM:c0 + QK_PAD], tab) * low
        ss = jnp.sum(nope * nope, axis=-1, keepdims=True) + jnp.sum(rope * rope, axis=-1, keepdims=True)
        rs = lax.rsqrt(ss * (1.0 / QK_DIM) + NORM_EPS)
        o_ref[:, c0:c0 + HEAD_DIM] = (nope * rs * g_nope).astype(o_ref.dtype)
        o_ref[:, c0 + HEAD_DIM:c0 + QK_PAD] = (rope * rs * g_rope).astype(o_ref.dtype)


def _ckv_kernel(a_ref, w_ref, tab_ref, g_ref, ckv_ref, ckvb_ref, kr_ref):
    acc = _dot(a_ref[...], w_ref[...])
    latent = ckv_ref.shape[1]
    cn = _rms(acc[:, :latent], g_ref[...])
    ckv_ref[...] = cn
    ckvb_ref[...] = cn.astype(ckvb_ref.dtype)
    kr_ref[...] = _rope_tile(acc[:, latent:], tab_ref[...])[:, :ROPE_DIM]


def _project(h, tab, w, tm=1024):
    m, d = h.shape
    tm = _tile(m, tm)
    width = HEADS * HEAD_DIM
    row = lambda i, j: (i, 0)
    full = lambda shape, dtype: (shape, dtype, (tm, shape[1]), row)
    (q,) = _matmul_call(_plain_kernel, h, w["sb_q"], [], [full((m, width), BF16)], tm, width, "proj_sb_q")
    k32, kb = _matmul_call(_plain_kernel, h, w["sb_k"], [], [full((m, width), F32), full((m, width), BF16)],
                           tm, width, "proj_sb_k")
    v32, vb = _matmul_call(_plain_kernel, h, w["sb_v"], [], [full((m, width), F32), full((m, width), BF16)],
                           tm, width, "proj_sb_v")
    tmq = _tile(m, 512)
    tab_spec = lambda t: (tab, (t, HEAD_DIM), row)
    gq = (w["q_norm_g"], (1, QK_PAD), lambda i, j: (0, 0))
    (qm,) = _matmul_call(_qmla_kernel, h, w["mla_q"], [tab_spec(tmq), gq],
                         [((m, HEADS * QK_PAD), BF16, (tmq, HEADS * QK_PAD), row)], tmq, HEADS * QK_PAD, "proj_mla_q")
    latent = w["kv_norm_g"].shape[1]
    gkv = (w["kv_norm_g"], (1, latent), lambda i, j: (0, 0))
    ckv32, ckvb, kr = _matmul_call(
        _ckv_kernel, h, w["ckv_kr"], [tab_spec(tm), gkv],
        [full((m, latent), F32), full((m, latent), BF16), full((m, ROPE_DIM), F32)], tm, latent + HEAD_DIM, "proj_ckv")
    tng = 1024
    (gate,) = _matmul_call(_gate_kernel, h, w["gate"], [], [((m, 2 * d), BF16, (tm, tng), lambda i, j: (i, j))],
                           tm, tng, "proj_gate")
    return dict(q=q, k32=k32, kb=kb, v32=v32, vb=vb, qm=qm, ckv32=ckv32, ckvb=ckvb, kr=kr, gate=gate)


def _kvproj_kernel(c_ref, kr_ref, wuk_ref, wuv_ref, g_ref, k_ref, v_ref):
    c = c_ref[...].astype(BF16)
    kn = _dot(c, wuk_ref[...])
    v_ref[...] = _dot(c, wuv_ref[...]).astype(v_ref.dtype)
    kr = kr_ref[...]
    ss_kr = jnp.sum(kr * kr, axis=-1, keepdims=True)
    kr_pad = jnp.concatenate([kr, jnp.zeros_like(kr)], axis=1)
    g_nope, g_rope = g_ref[:, :HEAD_DIM], g_ref[:, HEAD_DIM:]
    for h in range(HEADS):
        nope = kn[:, h * HEAD_DIM:(h + 1) * HEAD_DIM]
        ss = jnp.sum(nope * nope, axis=-1, keepdims=True) + ss_kr
        rs = lax.rsqrt(ss * (1.0 / QK_DIM) + NORM_EPS)
        c0 = h * QK_PAD
        k_ref[:, c0:c0 + HEAD_DIM] = (nope * rs * g_nope).astype(k_ref.dtype)
        k_ref[:, c0 + HEAD_DIM:c0 + QK_PAD] = (kr_pad * rs * g_rope).astype(k_ref.dtype)


def _kv_project(c, kr, w, rows, row_off=0, tm=512):
    latent = c.shape[1]
    tm = _tile(rows, tm)
    assert row_off % tm == 0
    off = row_off // tm
    width = HEADS * HEAD_DIM
    const = lambda i: (0, 0)
    return pl.pallas_call(
        _kvproj_kernel,
        grid=(rows // tm,),
        in_specs=[pl.BlockSpec((tm, latent), lambda i: (i + off, 0)),
                  pl.BlockSpec((tm, ROPE_DIM), lambda i: (i + off, 0)),
                  pl.BlockSpec((latent, width), const), pl.BlockSpec((latent, width), const),
                  pl.BlockSpec((1, QK_PAD), const)],
        out_specs=[pl.BlockSpec((tm, HEADS * QK_PAD), lambda i: (i, 0)), pl.BlockSpec((tm, width), lambda i: (i, 0))],
        out_shape=[jax.ShapeDtypeStruct((rows, HEADS * QK_PAD), BF16), jax.ShapeDtypeStruct((rows, width), BF16)],
        compiler_params=_params("parallel"),
        name="mla_kv_project",
    )(c, kr, w["uk"], w["uv"], w["k_norm_g"])


def _strict_lower(n):
    r = lax.broadcasted_iota(jnp.int32, (n, n), 0)
    c = lax.broadcasted_iota(jnp.int32, (n, n), 1)
    return jnp.where(r > c, 1.0, 0.0).astype(BF16)


def _sb_block(q, k, v, tri, rest, acc, mask):
    z = _nt_dot(q, k) * SB_SCALE
    log_1m = -(jnp.maximum(z, 0.0) + jnp.log(1.0 + jnp.exp(-jnp.abs(z))))
    if mask is not None:
        log_1m = jnp.where(mask, log_1m, 0.0)
    hi = log_1m.astype(BF16)
    lo = (log_1m - hi.astype(F32)).astype(BF16)
    later = _dot(hi, tri) + _dot(lo, tri)
    a = jnp.exp((z + log_1m) + later + rest)
    if mask is not None:
        a = jnp.where(mask, a, 0.0)
    acc = acc + _dot(a.astype(BF16), v)
    rest = rest + jnp.sum(log_1m, axis=-1, keepdims=True)
    return rest, acc


def _sb_prompt_kernel(q_ref, k_ref, v_ref, o_ref, *, blk):
    s_len = q_ref.shape[1]
    tri = _strict_lower(blk)
    t_idx = lax.broadcasted_iota(jnp.int32, (blk, blk), 0)
    s_idx = lax.broadcasted_iota(jnp.int32, (blk, blk), 1)
    causal = s_idx < t_idx

    def q_body(i, _):
        q0 = pl.multiple_of(i * blk, blk)
        q = q_ref[0, pl.ds(q0, blk), :]
        rest = jnp.zeros((blk, 1), F32)
        acc = jnp.zeros((blk, HEAD_DIM), F32)
        rest, acc = _sb_block(q, k_ref[0, pl.ds(q0, blk), :], v_ref[0, pl.ds(q0, blk), :], tri, rest, acc, causal)

        def k_body(jj, carry):
            k0 = pl.multiple_of((i - 1 - jj) * blk, blk)
            return _sb_block(q, k_ref[0, pl.ds(k0, blk), :], v_ref[0, pl.ds(k0, blk), :], tri, *carry, None)

        rest, acc = lax.fori_loop(0, i, k_body, (rest, acc))
        o_ref[0, pl.ds(q0, blk), :] = acc.astype(o_ref.dtype)
        return 0

    lax.fori_loop(0, s_len // blk, q_body, 0)


def _sb_prompt(q, k, v, blk=256):
    b, s, _ = q.shape
    blk = _tile(s, blk)
    spec = pl.BlockSpec((1, s, HEAD_DIM), lambda bi, h: (bi, 0, h))
    return pl.pallas_call(
        functools.partial(_sb_prompt_kernel, blk=blk),
        grid=(b, HEADS),
        in_specs=[spec, spec, spec],
        out_specs=spec,
        out_shape=jax.ShapeDtypeStruct(q.shape, BF16),
        compiler_params=_params("parallel", "parallel"),
        name="sb_prompt",
    )(q, k, v)


def _sb_sample_kernel(q_ref, kn_ref, vn_ref, kp_ref, vp_ref, o_ref, *, blk):
    n = q_ref.shape[1]
    past = kp_ref.shape[2]
    q = q_ref[0]
    t_idx = lax.broadcasted_iota(jnp.int32, (n, n), 0)
    s_idx = lax.broadcasted_iota(jnp.int32, (n, n), 1)
    rest = jnp.zeros((n, 1), F32)
    acc = jnp.zeros((n, HEAD_DIM), F32)
    rest, acc = _sb_block(q, kn_ref[0], vn_ref[0], _strict_lower(n), rest, acc, s_idx < t_idx)
    tri = _strict_lower(blk)
    nblk = past // blk

    def k_body(jj, carry):
        k0 = pl.multiple_of((nblk - 1 - jj) * blk, blk)
        k = kp_ref[0, 0, pl.ds(k0, blk), :].astype(BF16)
        v = vp_ref[0, 0, pl.ds(k0, blk), :].astype(BF16)
        return _sb_block(q, k, v, tri, *carry, None)

    rest, acc = lax.fori_loop(0, nblk, k_body, (rest, acc))
    o_ref[0] = acc.astype(o_ref.dtype)


def _sb_sample(q, k_new, v_new, cache_k, cache_v, layer, blk=256):
    b, n, _ = q.shape
    past = cache_k.shape[2]
    blk = _tile(past, blk)
    new = pl.BlockSpec((1, n, HEAD_DIM), lambda bi, h: (bi, 0, h))
    old = pl.BlockSpec((1, 1, past, HEAD_DIM), lambda bi, h: (layer, bi, 0, h))
    return pl.pallas_call(
        functools.partial(_sb_sample_kernel, blk=blk),
        grid=(b, HEADS),
        in_specs=[new, new, new, old, old],
        out_specs=new,
        out_shape=jax.ShapeDtypeStruct(q.shape, BF16),
        compiler_params=_params("parallel", "parallel"),
        name="sb_sample",
    )(q, k_new, v_new, cache_k, cache_v)


def _mla_block(q, k, v, m, l, acc, mask):
    s = _nt_dot(q, k) * MLA_SCALE
    if mask is not None:
        s = jnp.where(mask, s, NEG_INF)
    m_new = jnp.maximum(m, jnp.max(s, axis=-1, keepdims=True))
    alpha = jnp.exp(m - m_new)
    p = jnp.exp(s - m_new)
    l = alpha * l + jnp.sum(p, axis=-1, keepdims=True)
    acc = alpha * acc + _dot(p.astype(BF16), v)
    return m_new, l, acc


def _mla_prompt_kernel(q_ref, k_ref, v_ref, o_ref, *, blk):
    s_len = q_ref.shape[1]
    t_idx = lax.broadcasted_iota(jnp.int32, (blk, blk), 0)
    s_idx = lax.broadcasted_iota(jnp.int32, (blk, blk), 1)
    chunk_mask = (s_idx // CHUNK) <= (t_idx // CHUNK)

    def q_body(i, _):
        q0 = pl.multiple_of(i * blk, blk)
        q = q_ref[0, pl.ds(q0, blk), :]

        def k_body(j, carry):
            k0 = pl.multiple_of(j * blk, blk)
            return _mla_block(q, k_ref[0, pl.ds(k0, blk), :], v_ref[0, pl.ds(k0, blk), :], *carry, None)

        init = (jnp.full((blk, 1), NEG_INF, F32), jnp.zeros((blk, 1), F32), jnp.zeros((blk, HEAD_DIM), F32))
        carry = lax.fori_loop(0, i, k_body, init)
        _, l, acc = _mla_block(q, k_ref[0, pl.ds(q0, blk), :], v_ref[0, pl.ds(q0, blk), :], *carry, chunk_mask)
        o_ref[0, pl.ds(q0, blk), :] = (acc / l).astype(o_ref.dtype)
        return 0

    lax.fori_loop(0, s_len // blk, q_body, 0)


def _mla_prompt(q, k, v, blk=256):
    b, s, _ = q.shape
    blk = _tile(s, blk)
    assert blk % CHUNK == 0
    qk = pl.BlockSpec((1, s, QK_PAD), lambda bi, h: (bi, 0, h))
    vo = pl.BlockSpec((1, s, HEAD_DIM), lambda bi, h: (bi, 0, h))
    return pl.pallas_call(
        functools.partial(_mla_prompt_kernel, blk=blk),
        grid=(b, HEADS),
        in_specs=[qk, qk, vo],
        out_specs=vo,
        out_shape=jax.ShapeDtypeStruct(v.shape, BF16),
        compiler_params=_params("parallel", "parallel"),
        name="mla_prompt",
    )(q, k, v)


def _mla_sample_kernel(q_ref, kp_ref, vp_ref, kn_ref, vn_ref, o_ref, *, past):
    n = q_ref.shape[1]
    q = q_ref[0]
    s_past = _nt_dot(q, kp_ref[0]) * MLA_SCALE
    s_new = _nt_dot(q, kn_ref[0]) * MLA_SCALE
    t_pos = past + lax.broadcasted_iota(jnp.int32, (n, n), 0)
    s_pos = past + lax.broadcasted_iota(jnp.int32, (n, n), 1)
    s_new = jnp.where((s_pos // CHUNK) <= (t_pos // CHUNK), s_new, NEG_INF)
    m = jnp.maximum(jnp.max(s_past, axis=-1, keepdims=True), jnp.max(s_new, axis=-1, keepdims=True))
    p_past = jnp.exp(s_past - m)
    p_new = jnp.exp(s_new - m)
    l = jnp.sum(p_past, axis=-1, keepdims=True) + jnp.sum(p_new, axis=-1, keepdims=True)
    acc = _dot(p_past.astype(BF16), vp_ref[0]) + _dot(p_new.astype(BF16), vn_ref[0])
    o_ref[0] = (acc / l).astype(o_ref.dtype)


def _mla_sample(q, k_past, v_past, k_new, v_new):
    b, n, _ = q.shape
    past = k_past.shape[1]
    spec = lambda rows, width: pl.BlockSpec((1, rows, width), lambda bi, h: (bi, 0, h))
    return pl.pallas_call(
        functools.partial(_mla_sample_kernel, past=past),
        grid=(b, HEADS),
        in_specs=[spec(n, QK_PAD), spec(past, QK_PAD), spec(past, HEAD_DIM), spec(n, QK_PAD), spec(n, HEAD_DIM)],
        out_specs=spec(n, HEAD_DIM),
        out_shape=jax.ShapeDtypeStruct(v_new.shape, BF16),
        compiler_params=_params("parallel", "parallel"),
        name="mla_sample",
    )(q, k_past, v_past, k_new, v_new)


def _merge_kernel(a1_ref, a2_ref, w1_ref, w2_ref, g1_ref, g2_ref, o_ref):
    y1 = _dot(a1_ref[...], w1_ref[...])
    y2 = _dot(a2_ref[...], w2_ref[...])
    o_ref[...] = (g1_ref[...].astype(F32) * y1 + g2_ref[...].astype(F32) * y2).astype(o_ref.dtype)


def _merge(sb_out, mla_out, gate, w, tm=1024, tn=1024):
    m, k = sb_out.shape
    d = w["sb_proj"].shape[1]
    tm, tn = _tile(m, tm), _tile(d, tn)
    nj = d // tn
    a_spec = pl.BlockSpec((tm, k), lambda i, j: (i, 0))
    w_spec = pl.BlockSpec((k, tn), lambda i, j: (0, j))
    return pl.pallas_call(
        _merge_kernel,
        grid=(m // tm, nj),
        in_specs=[a_spec, a_spec, w_spec, w_spec,
                  pl.BlockSpec((tm, tn), lambda i, j: (i, j)), pl.BlockSpec((tm, tn), lambda i, j: (i, j + nj))],
        out_specs=pl.BlockSpec((tm, tn), lambda i, j: (i, j)),
        out_shape=jax.ShapeDtypeStruct((m, d), BF16),
        compiler_params=_params("parallel", "arbitrary"),
        name="merge",
    )(sb_out, mla_out, w["sb_proj"], w["mla_proj"], gate, gate)


def _wo_kernel(a_ref, w_ref, x_ref, o_ref):
    o_ref[...] = x_ref[...] + _dot(a_ref[...], w_ref[...])


def _wo_residual(mix, x, w, tm=1024, tn=1024):
    m, d = x.shape
    tile = (_tile(m, tm), _tile(d, tn))
    ij = lambda i, j: (i, j)
    (out,) = _matmul_call(_wo_kernel, mix, w["o"], [(x, tile, ij)], [((m, d), F32, tile, ij)], tm, tn, "wo_residual")
    return out


def _ffn_kernel(x_ref, g_ref, wup_ref, wdn_ref, o_ref, h_ref):
    @pl.when(pl.program_id(1) == 0)
    def _():
        x = x_ref[...]
        h_ref[...] = _rms(x, g_ref[...]).astype(h_ref.dtype)
        o_ref[...] = x

    u = _dot(h_ref[...], wup_ref[...])
    u = jnp.square(jnp.maximum(u, 0.0)).astype(BF16)
    o_ref[...] += _dot(u, wdn_ref[...])


def _ffn(x, w, tm=1024, tf=512):
    m, d = x.shape
    f = w["up"].shape[1]
    tm, tf = _tile(m, tm), _tile(f, tf)
    return pl.pallas_call(
        _ffn_kernel,
        grid=(m // tm, f // tf),
        in_specs=[pl.BlockSpec((tm, d), lambda i, j: (i, 0)), pl.BlockSpec((1, d), lambda i, j: (0, 0)),
                  pl.BlockSpec((d, tf), lambda i, j: (0, j)), pl.BlockSpec((tf, d), lambda i, j: (j, 0))],
        out_specs=pl.BlockSpec((tm, d), lambda i, j: (i, 0)),
        out_shape=jax.ShapeDtypeStruct((m, d), F32),
        scratch_shapes=[pltpu.VMEM((tm, d), BF16)],
        compiler_params=_params("parallel", "arbitrary"),
        name="ffn",
    )(x, w["norm2_g"], w["up"], w["down"])


def _rope_table(pos):
    inv_freq = ROPE_THETA ** (-jnp.arange(HALF_ROPE, dtype=F32) / HALF_ROPE)
    ang = pos.astype(F32)[:, None] * inv_freq[None, :]
    cos, sin = jnp.cos(ang), jnp.sin(ang)
    return jnp.concatenate([cos, cos, -sin, sin], axis=1)


def _rope_cols(w):
    x1, x2 = w[..., :HALF_ROPE], w[..., HALF_ROPE:]
    return jnp.concatenate([x1, x2, x2, x1], axis=-1)


def _pad_gain(g):
    return jnp.concatenate([g, jnp.zeros((QK_PAD - QK_DIM,), g.dtype)]).reshape(1, QK_PAD)


def _layer_weights(l, norm1_g, w_in, q_norm_g, k_norm_g, kv_norm_g, w_uk, w_uv, w_sb_proj, w_mla_proj, w_o,
                   norm2_g, w_up, w_down):
    d = w_in.shape[1]
    width = HEADS * HEAD_DIM
    latent = w_uk.shape[1]
    wi = w_in[l]
    o_q, o_ckv = 3 * width, 3 * width + HEADS * QK_DIM
    o_kr, o_gate = o_ckv + latent, o_ckv + latent + ROPE_DIM
    wq = wi[:, o_q:o_ckv].reshape(d, HEADS, QK_DIM)
    wq = jnp.concatenate([wq[..., :HEAD_DIM], _rope_cols(wq[..., HEAD_DIM:])], axis=-1).reshape(d, HEADS * QK_PAD)
    bf = lambda a: a.astype(BF16)
    return {
        "norm1_g": norm1_g[l], "norm2_g": norm2_g[l].reshape(1, d),
        "sb_q": bf(wi[:, :width]), "sb_k": bf(wi[:, width:2 * width]), "sb_v": bf(wi[:, 2 * width:o_q]),
        "mla_q": bf(wq),
        "ckv_kr": bf(jnp.concatenate([wi[:, o_ckv:o_kr], _rope_cols(wi[:, o_kr:o_gate])], axis=1)),
        "gate": bf(wi[:, o_gate:]),
        "q_norm_g": _pad_gain(q_norm_g[l]), "k_norm_g": _pad_gain(k_norm_g[l]),
        "kv_norm_g": kv_norm_g[l].reshape(1, latent),
        "uk": bf(w_uk[l]), "uv": bf(w_uv[l]),
        "sb_proj": bf(w_sb_proj[l]), "mla_proj": bf(w_mla_proj[l]), "o": bf(w_o[l]),
        "up": bf(w_up[l]), "down": bf(w_down[l]),
    }


def _merge_and_ffn(x, sb_out, mla_out, gate, w):
    mix = _merge(sb_out, mla_out, gate, w)
    return _ffn(_wo_residual(mix, x, w), w)


def kernel(x_prompt, x_sample, cache_sb_k, cache_sb_v, cache_mla_ckv, cache_mla_krope, norm1_g, w_in, q_norm_g, k_norm_g, kv_norm_g, w_uk, w_uv, w_sb_proj, w_mla_proj, w_o, norm2_g, w_up, w_down):
    b, s, d = x_prompt.shape
    bs, n, _ = x_sample.shape
    depth, _, past = cache_sb_k.shape[:3]
    width = HEADS * HEAD_DIM
    latent = cache_mla_ckv.shape[-1]

    tab_p = jnp.tile(_rope_table(jnp.arange(s, dtype=jnp.int32)), (b, 1))
    tab_s = jnp.tile(_rope_table(past + jnp.arange(n, dtype=jnp.int32)), (bs, 1))
    cache_k = cache_sb_k.reshape(depth, bs, past, width)
    cache_v = cache_sb_v.reshape(depth, bs, past, width)
    cache_c = cache_mla_ckv.reshape(depth * bs * past, latent)
    cache_r = cache_mla_krope.reshape(depth * bs * past, ROPE_DIM)

    xp = x_prompt.reshape(b * s, d)
    xs = x_sample.reshape(bs * n, d)
    outs = [[] for _ in range(8)]
    for l in range(depth):
        w = _layer_weights(l, norm1_g, w_in, q_norm_g, k_norm_g, kv_norm_g, w_uk, w_uv, w_sb_proj, w_mla_proj,
                           w_o, norm2_g, w_up, w_down)
        p = _project(_rmsnorm(xp, w["norm1_g"]), tab_p, w)
        seq = lambda a: a.reshape(b, s, a.shape[1])
        sb_out = _sb_prompt(seq(p["q"]), seq(p["kb"]), seq(p["vb"]))
        k_mla, v_mla = _kv_project(p["ckvb"], p["kr"], w, b * s)
        mla_out = _mla_prompt(seq(p["qm"]), seq(k_mla), seq(v_mla))
        xp = _merge_and_ffn(xp, sb_out.reshape(b * s, width), mla_out.reshape(b * s, width), p["gate"], w)
        outs[0].append(p["k32"].reshape(b, s, HEADS, HEAD_DIM))
        outs[1].append(p["v32"].reshape(b, s, HEADS, HEAD_DIM))
        outs[2].append(p["ckv32"].reshape(b, s, latent))
        outs[3].append(p["kr"].reshape(b, s, ROPE_DIM))
        q = _project(_rmsnorm(xs, w["norm1_g"]), tab_s, w)
        dec = lambda a: a.reshape(bs, n, a.shape[1])
        sb_out = _sb_sample(dec(q["q"]), dec(q["kb"]), dec(q["vb"]), cache_k, cache_v, l)
        kp, vp = _kv_project(cache_c, cache_r, w, bs * past, row_off=l * bs * past)
        kn, vn = _kv_project(q["ckvb"], q["kr"], w, bs * n)
        mla_out = _mla_sample(dec(q["qm"]), kp.reshape(bs, past, -1), vp.reshape(bs, past, -1), dec(kn), dec(vn))
        xs = _merge_and_ffn(xs, sb_out.reshape(bs * n, width), mla_out.reshape(bs * n, width), q["gate"], w)
        outs[4].append(q["k32"].reshape(bs, n, HEADS, HEAD_DIM))
        outs[5].append(q["v32"].reshape(bs, n, HEADS, HEAD_DIM))
        outs[6].append(q["ckv32"].reshape(bs, n, latent))
        outs[7].append(q["kr"].reshape(bs, n, ROPE_DIM))
    return (xp.reshape(b, s, d), xs.reshape(bs, n, d), *[jnp.stack(o) for o in outs])
```

```python
import functools

import jax
import jax.numpy as jnp
from jax import lax
from jax.experimental import pallas as pl
from jax.experimental.pallas import tpu as pltpu

F32 = jnp.float32
BF16 = jnp.bfloat16

CHUNK = 64
HEADS = 8
HEAD_DIM = 128
ROPE_DIM = 64
HALF_ROPE = ROPE_DIM // 2
QK_DIM = HEAD_DIM + ROPE_DIM
QK_PAD = 2 * HEAD_DIM
ROPE_THETA = 10000.0
NORM_EPS = 1e-6
NEG_INF = -1e30
SB_SCALE = HEAD_DIM ** -0.5
MLA_SCALE = QK_DIM ** -0.5
SB_DONE = 110.0

VMEM_LIMIT_BYTES = 56 * 1024 * 1024


def _params(*sem):
    return pltpu.CompilerParams(dimension_semantics=sem, vmem_limit_bytes=VMEM_LIMIT_BYTES)


def _tile(m, pref):
    t = min(m, pref)
    assert m % t == 0, (m, pref)
    return t


def _nt_dot(a, b):
    return lax.dot_general(a, b, (((1,), (1,)), ((), ())), preferred_element_type=F32)


def _dot(a, b):
    return jnp.dot(a, b, preferred_element_type=F32)


def _rms(x, g):
    ms = jnp.mean(x * x, axis=-1, keepdims=True)
    return x * lax.rsqrt(ms + NORM_EPS) * g


def _rmsnorm_kernel(x_ref, g_ref, o_ref):
    o_ref[...] = _rms(x_ref[...], g_ref[...]).astype(o_ref.dtype)


def _rmsnorm(x, g, tm=512):
    m, d = x.shape
    tm = _tile(m, tm)
    return pl.pallas_call(
        _rmsnorm_kernel,
        grid=(m // tm,),
        in_specs=[pl.BlockSpec((tm, d), lambda i: (i, 0)), pl.BlockSpec((1, d), lambda i: (0, 0))],
        out_specs=pl.BlockSpec((tm, d), lambda i: (i, 0)),
        out_shape=jax.ShapeDtypeStruct((m, d), BF16),
        compiler_params=_params("parallel"),
        name="rmsnorm",
    )(x, g.reshape(1, d))


def _matmul_call(body, a, w, extras, outs, tm, tn, name, carried=()):
    m, k = a.shape
    n = w.shape[1]
    tm, tn = _tile(m, tm), _tile(n, tn)
    n_in = 2 + len(extras)
    in_specs = [pl.BlockSpec((tm, k), lambda i, j: (i, 0)), pl.BlockSpec((k, tn), lambda i, j: (0, j))]
    in_specs += [pl.BlockSpec(bs, im) for _, bs, im in extras]
    in_specs += [pl.BlockSpec(memory_space=pl.ANY)] * len(carried)

    def kernel_fn(*refs):
        body(*refs[:n_in], *refs[n_in + len(carried):])

    return pl.pallas_call(
        kernel_fn,
        grid=(m // tm, n // tn),
        in_specs=in_specs,
        out_specs=[pl.BlockSpec(bs, im) for _, _, bs, im in outs],
        out_shape=[jax.ShapeDtypeStruct(s, d) for s, d, _, _ in outs],
        input_output_aliases={n_in + t: oi for t, (_, oi) in enumerate(carried)},
        compiler_params=_params("parallel", "arbitrary"),
        name=name,
    )(a, w, *[e[0] for e in extras], *[c[0] for c in carried])


def _sb_q_kernel(a_ref, w_ref, o_ref):
    o_ref[...] = (_dot(a_ref[...], w_ref[...]) * SB_SCALE).astype(o_ref.dtype)


def _sb_kv_kernel(a_ref, w_ref, cache_ref, o_ref):
    acc = _dot(a_ref[...], w_ref[...])
    o_ref[...] = acc.astype(o_ref.dtype)
    tm = acc.shape[0]
    for h in range(HEADS):
        cache_ref[pl.ds(h, tm, stride=HEADS), :] = acc[:, h * HEAD_DIM:(h + 1) * HEAD_DIM]


def _gate_kernel(a_ref, w_ref, o_ref):
    o_ref[...] = jax.nn.sigmoid(_dot(a_ref[...], w_ref[...])).astype(o_ref.dtype)


def _rope_tile(x, tab):
    t = x * tab
    return t + pltpu.roll(t, ROPE_DIM, 1)


def _qmla_kernel(a_ref, w_ref, tab_ref, g_ref, o_ref):
    acc = _dot(a_ref[...], w_ref[...])
    tab = tab_ref[...]
    low = (lax.broadcasted_iota(jnp.int32, (1, HEAD_DIM), 1) < ROPE_DIM).astype(F32)
    g_nope, g_rope = g_ref[:, :HEAD_DIM], g_ref[:, HEAD_DIM:]
    for h in range(HEADS):
        c0 = h * QK_PAD
        nope = acc[:, c0:c0 + HEAD_DIM]
        rope = _rope_tile(acc[:, c0 + HEAD_DIM:c0 + QK_PAD], tab) * low
        ss = jnp.sum(nope * nope, axis=-1, keepdims=True) + jnp.sum(rope * rope, axis=-1, keepdims=True)
        rs = lax.rsqrt(ss * (1.0 / QK_DIM) + NORM_EPS)
        o_ref[:, c0:c0 + HEAD_DIM] = (nope * rs * g_nope * MLA_SCALE).astype(o_ref.dtype)
        o_ref[:, c0 + HEAD_DIM:c0 + QK_PAD] = (rope * rs * g_rope * MLA_SCALE).astype(o_ref.dtype)


def _ckv_kernel(a_ref, w_ref, tab_ref, g_ref, ckv_ref, kr_ref, ckvb_ref):
    acc = _dot(a_ref[...], w_ref[...])
    latent = ckv_ref.shape[1]
    cn = _rms(acc[:, :latent], g_ref[...])
    ckv_ref[...] = cn
    ckvb_ref[...] = cn.astype(ckvb_ref.dtype)
    kr_ref[...] = _rope_tile(acc[:, latent:], tab_ref[...])[:, :ROPE_DIM]


def _project(h, tab, w, layer, new_cache, tm=1024):
    m, d = h.shape
    tm = _tile(m, tm)
    width = HEADS * HEAD_DIM
    depth = w["depth"]
    latent = w["kv_norm_g"].shape[1]
    row = lambda i, j: (i, 0)
    lrow = lambda i, j: (layer, i, 0)
    full = lambda shape, dtype: (shape, dtype, (tm, shape[1]), row)
    carry = lambda idx, out: [] if new_cache is None else [(new_cache[idx], out)]

    (q,) = _matmul_call(_sb_q_kernel, h, w["sb_q"], [], [full((m, width), BF16)], tm, width, "proj_sb_q")
    cache_out = ((depth, m * HEADS, HEAD_DIM), F32, (None, tm * HEADS, HEAD_DIM), lrow)
    k32, kb = _matmul_call(_sb_kv_kernel, h, w["sb_k"], [], [cache_out, full((m, width), BF16)],
                           tm, width, "proj_sb_k", carry(0, 0))
    v32, vb = _matmul_call(_sb_kv_kernel, h, w["sb_v"], [], [cache_out, full((m, width), BF16)],
                           tm, width, "proj_sb_v", carry(1, 0))
    tmq = _tile(m, 512)
    tab_spec = lambda t: (tab, (t, HEAD_DIM), row)
    gq = (w["q_norm_g"], (1, QK_PAD), lambda i, j: (0, 0))
    (qm,) = _matmul_call(_qmla_kernel, h, w["mla_q"], [tab_spec(tmq), gq],
                         [((m, HEADS * QK_PAD), BF16, (tmq, HEADS * QK_PAD), row)], tmq, HEADS * QK_PAD, "proj_mla_q")
    gkv = (w["kv_norm_g"], (1, latent), lambda i, j: (0, 0))
    ckv32, kr, ckvb = _matmul_call(
        _ckv_kernel, h, w["ckv_kr"], [tab_spec(tm), gkv],
        [((depth, m, latent), F32, (None, tm, latent), lrow), ((depth, m, ROPE_DIM), F32, (None, tm, ROPE_DIM), lrow),
         full((m, latent), BF16)], tm, latent + HEAD_DIM, "proj_ckv", carry(2, 0) + carry(3, 1))
    tng = 1024
    (gate,) = _matmul_call(_gate_kernel, h, w["gate"], [], [((m, 2 * d), BF16, (tm, tng), lambda i, j: (i, j))],
                           tm, tng, "proj_gate")
    return dict(q=q, kb=kb, vb=vb, qm=qm, ckvb=ckvb, gate=gate), (k32, v32, ckv32, kr)


def _kvproj_kernel(c_ref, kr_ref, wuk_ref, wuv_ref, g_ref, k_ref, v_ref):
    c = c_ref[...].astype(BF16)
    kn = _dot(c, wuk_ref[...])
    v_ref[...] = _dot(c, wuv_ref[...]).astype(v_ref.dtype)
    kr = kr_ref[...]
    ss_kr = jnp.sum(kr * kr, axis=-1, keepdims=True)
    kr_pad = jnp.concatenate([kr, jnp.zeros_like(kr)], axis=1)
    g_nope, g_rope = g_ref[:, :HEAD_DIM], g_ref[:, HEAD_DIM:]
    for h in range(HEADS):
        nope = kn[:, h * HEAD_DIM:(h + 1) * HEAD_DIM]
        ss = jnp.sum(nope * nope, axis=-1, keepdims=True) + ss_kr
        rs = lax.rsqrt(ss * (1.0 / QK_DIM) + NORM_EPS)
        c0 = h * QK_PAD
        k_ref[:, c0:c0 + HEAD_DIM] = (nope * rs * g_nope).astype(k_ref.dtype)
        k_ref[:, c0 + HEAD_DIM:c0 + QK_PAD] = (kr_pad * rs * g_rope).astype(k_ref.dtype)


def _kv_project(c, c_layer, kr, kr_layer, w, tm=512):
    _, rows, latent = c.shape
    tm = _tile(rows, tm)
    width = HEADS * HEAD_DIM
    const = lambda i: (0, 0)
    return pl.pallas_call(
        _kvproj_kernel,
        grid=(rows // tm,),
        in_specs=[pl.BlockSpec((None, tm, latent), lambda i: (c_layer, i, 0)),
                  pl.BlockSpec((None, tm, ROPE_DIM), lambda i: (kr_layer, i, 0)),
                  pl.BlockSpec((latent, width), const), pl.BlockSpec((latent, width), const),
                  pl.BlockSpec((1, QK_PAD), const)],
        out_specs=[pl.BlockSpec((tm, HEADS * QK_PAD), lambda i: (i, 0)), pl.BlockSpec((tm, width), lambda i: (i, 0))],
        out_shape=[jax.ShapeDtypeStruct((rows, HEADS * QK_PAD), BF16), jax.ShapeDtypeStruct((rows, width), BF16)],
        compiler_params=_params("parallel"),
        name="mla_kv_project",
    )(c, kr, w["uk"], w["uv"], w["k_norm_g"])


def _later_sum_matrix(n, doubled):
    rows = 2 * n if doubled else n
    r = lax.broadcasted_iota(jnp.int32, (rows, n), 0)
    c = lax.broadcasted_iota(jnp.int32, (rows, n), 1)
    if doubled:
        r = jnp.where(r >= n, r - n, r)
    return jnp.where(r > c, 1.0, 0.0).astype(BF16)


def _sb_block(q, k, v, tri, rest, acc, mask):
    z = _nt_dot(q, k)
    sp = jnp.maximum(z, 0.0) + jnp.log(1.0 + jnp.exp(-jnp.abs(z)))
    if mask is not None:
        sp = jnp.where(mask, sp, 0.0)
    hi = sp.astype(BF16)
    lo = (sp - hi.astype(F32)).astype(BF16)
    if tri.shape[0] == 2 * tri.shape[1]:
        later = _dot(jnp.concatenate([hi, lo], axis=1), tri)
    else:
        later = _dot(hi, tri) + _dot(lo, tri)
    a = jnp.exp(((z - sp) - later) - rest)
    if mask is not None:
        a = jnp.where(mask, a, 0.0)
    acc = acc + _dot(a.astype(BF16), v)
    rest = rest + jnp.sum(sp, axis=-1, keepdims=True)
    return rest, acc


def _unfinished(rest):
    return (jnp.min(rest) < SB_DONE).astype(jnp.int32)


def _causal_mask(n):
    t_idx = lax.broadcasted_iota(jnp.int32, (n, n), 0)
    s_idx = lax.broadcasted_iota(jnp.int32, (n, n), 1)
    return s_idx < t_idx


def _sb_prompt_kernel(q_ref, k_ref, v_ref, o_ref, *, blk, group):
    s_len = q_ref.shape[1]
    tri = _later_sum_matrix(blk, blk % HEAD_DIM == 0)
    causal = _causal_mask(blk)

    def kv(j0):
        return k_ref[0, pl.ds(j0, blk), :], v_ref[0, pl.ds(j0, blk), :]

    def nearest(q0, has_prev):
        q = q_ref[0, pl.ds(q0, blk), :]
        rest, acc = _sb_block(q, *kv(q0), tri, jnp.zeros((blk, 1), F32), jnp.zeros((blk, HEAD_DIM), F32), causal)
        if has_prev:
            rest, acc = _sb_block(q, *kv(q0 - blk), tri, rest, acc, None)
        return q, rest, acc

    def further(q, j_first, rest, acc):
        def cond(c):
            return jnp.logical_and(c[0] >= 0, c[1] > 0)

        def body(c):
            j, _, rest, acc = c
            rest, acc = _sb_block(q, *kv(pl.multiple_of(j * blk, blk)), tri, rest, acc, None)
            return j - 1, _unfinished(rest), rest, acc

        return lax.while_loop(cond, body, (j_first, _unfinished(rest), rest, acc))[3]

    def do_group(first, static):
        idx, q0s, states = [], [], []
        for g in range(group):
            i = first + g
            q0 = i * blk if static else pl.multiple_of(i * blk, blk)
            idx.append(i)
            q0s.append(q0)
            states.append(nearest(q0, i > 0 if static else True))
        unfinished = functools.reduce(jnp.maximum, [_unfinished(rest) for _, rest, _ in states])
        accs = lax.cond(
            unfinished > 0,
            lambda: tuple(further(q, i - 2, rest, acc) for i, (q, rest, acc) in zip(idx, states)),
            lambda: tuple(acc for _, _, acc in states))
        for q0, acc in zip(q0s, accs):
            o_ref[0, pl.ds(q0, blk), :] = acc.astype(o_ref.dtype)

    do_group(0, True)

    def group_body(gi, _):
        do_group(gi * group, False)
        return 0

    lax.fori_loop(1, s_len // (blk * group), group_body, 0)


def _sb_prompt(q, k, v, blk=256, group=2):
    b, s, _ = q.shape
    blk = _tile(s, blk)
    group = _tile(s // blk, group)
    spec = pl.BlockSpec((1, s, HEAD_DIM), lambda bi, h: (bi, 0, h))
    return pl.pallas_call(
        functools.partial(_sb_prompt_kernel, blk=blk, group=group),
        grid=(b, HEADS),
        in_specs=[spec, spec, spec],
        out_specs=spec,
        out_shape=jax.ShapeDtypeStruct(q.shape, BF16),
        compiler_params=_params("parallel", "parallel"),
        name="sb_prompt",
    )(q, k, v)


def _sb_sample_kernel(q_ref, kn_ref, vn_ref, kp_ref, vp_ref, o_ref, rest_ref, acc_ref, live_ref, *, blk):
    c = pl.program_id(1)
    n = q_ref.shape[1]
    chunk = kp_ref.shape[0] // HEADS
    cols = lambda h: slice(h * HEAD_DIM, (h + 1) * HEAD_DIM)

    @pl.when(c == 0)
    def _():
        tri = _later_sum_matrix(n, n % HEAD_DIM == 0)
        causal = _causal_mask(n)
        live = jnp.int32(0)
        for h in range(HEADS):
            rest, acc = _sb_block(q_ref[0, :, cols(h)], kn_ref[0, :, cols(h)], vn_ref[0, :, cols(h)], tri,
                                  jnp.zeros((n, 1), F32), jnp.zeros((n, HEAD_DIM), F32), causal)
            rest_ref[h] = rest
            acc_ref[h] = acc
            live = jnp.maximum(live, _unfinished(rest))
        live_ref[0] = live

    @pl.when(live_ref[0] > 0)
    def _():
        tri = _later_sum_matrix(blk, blk % HEAD_DIM == 0)
        live = jnp.int32(0)
        for h in range(HEADS):
            q = q_ref[0, :, cols(h)]
            rest, acc = rest_ref[h], acc_ref[h]
            for jb in reversed(range(chunk // blk)):
                rows = pl.ds(jb * blk * HEADS + h, blk, stride=HEADS)
                rest, acc = _sb_block(q, kp_ref[rows, :].astype(BF16), vp_ref[rows, :].astype(BF16), tri, rest, acc, None)
            rest_ref[h] = rest
            acc_ref[h] = acc
            live = jnp.maximum(live, _unfinished(rest))
        live_ref[0] = live

    @pl.when(c == pl.num_programs(1) - 1)
    def _():
        for h in range(HEADS):
            o_ref[0, :, cols(h)] = acc_ref[h].astype(o_ref.dtype)


def _sb_sample(q, k_new, v_new, cache_k, cache_v, layer, chunk=512, blk=256):
    b, n, width = q.shape
    past = cache_k.shape[2] // HEADS
    chunk = _tile(past, chunk)
    blk = _tile(chunk, blk)
    nc = past // chunk
    new = pl.BlockSpec((1, n, width), lambda bi, c: (bi, 0, 0))
    old = pl.BlockSpec((None, None, chunk * HEADS, HEAD_DIM), lambda bi, c: (layer, bi, nc - 1 - c, 0))
    return pl.pallas_call(
        functools.partial(_sb_sample_kernel, blk=blk),
        grid=(b, nc),
        in_specs=[new, new, new, old, old],
        out_specs=new,
        out_shape=jax.ShapeDtypeStruct(q.shape, BF16),
        scratch_shapes=[pltpu.VMEM((HEADS, n, 1), F32), pltpu.VMEM((HEADS, n, HEAD_DIM), F32),
                        pltpu.SMEM((1,), jnp.int32)],
        compiler_params=_params("parallel", "arbitrary"),
        name="sb_sample",
    )(q, k_new, v_new, cache_k, cache_v)


def _mla_advance(state, s, v_prev, mask):
    m, l, acc, p_prev = state
    pv = _dot(p_prev, v_prev)
    if mask is not None:
        s = jnp.where(mask, s, NEG_INF)
    m_new = jnp.maximum(m, jnp.max(s, axis=-1, keepdims=True))
    alpha = jnp.exp(m - m_new)
    p = jnp.exp(s - m_new)
    l = alpha * l + jnp.sum(p, axis=-1, keepdims=True)
    return m_new, l, alpha * (acc + pv), p.astype(BF16)


def _mla_finish(state, v_last):
    _, l, acc, p = state
    return (acc + _dot(p, v_last)) / l


def _mla_prompt_kernel(q_ref, k_ref, v_ref, o_ref, *, blk):
    s_len = q_ref.shape[1]
    t_idx = lax.broadcasted_iota(jnp.int32, (blk, blk), 0)
    s_idx = lax.broadcasted_iota(jnp.int32, (blk, blk), 1)
    chunk_mask = (s_idx // CHUNK) <= (t_idx // CHUNK)
    init = (jnp.full((blk, 1), NEG_INF, F32), jnp.zeros((blk, 1), F32), jnp.zeros((blk, HEAD_DIM), F32),
            jnp.zeros((blk, blk), BF16))

    def rows(j):
        return pl.ds(j * blk if isinstance(j, int) else pl.multiple_of(j * blk, blk), blk)

    def keys(j):
        return k_ref[0, rows(j), :]

    def values(j):
        return v_ref[0, rows(j), :]

    def pair_body(pi, _):
        ja = 2 * pi
        qa0 = pl.multiple_of(ja * blk, 2 * blk)
        q_pair = q_ref[0, pl.ds(qa0, 2 * blk), :]

        def k_body(j, carry):
            sa, sb, s = carry
            s_next = _nt_dot(q_pair, keys(j + 1))
            v_prev = values(jnp.maximum(j - 1, 0))
            return _mla_advance(sa, s[:blk], v_prev, None), _mla_advance(sb, s[blk:], v_prev, None), s_next

        sa, sb, s = lax.fori_loop(0, ja, k_body, (init, init, _nt_dot(q_pair, keys(0))))
        s_last = _nt_dot(q_pair[blk:], keys(ja + 1))
        v_prev = values(jnp.maximum(ja - 1, 0))
        sa = _mla_advance(sa, s[:blk], v_prev, chunk_mask)
        sb = _mla_advance(sb, s[blk:], v_prev, None)
        o_ref[0, pl.ds(qa0, blk), :] = _mla_finish(sa, values(ja)).astype(o_ref.dtype)
        sb = _mla_advance(sb, s_last, values(ja), chunk_mask)
        o_ref[0, pl.ds(qa0 + blk, blk), :] = _mla_finish(sb, values(ja + 1)).astype(o_ref.dtype)
        return 0

    lax.fori_loop(0, s_len // (2 * blk), pair_body, 0)


def _mla_prompt(q, k, v, blk=256):
    b, s, _ = q.shape
    blk = _tile(s // 2, blk)
    assert blk % CHUNK == 0
    qk = pl.BlockSpec((1, s, QK_PAD), lambda bi, h: (bi, 0, h))
    vo = pl.BlockSpec((1, s, HEAD_DIM), lambda bi, h: (bi, 0, h))
    return pl.pallas_call(
        functools.partial(_mla_prompt_kernel, blk=blk),
        grid=(b, HEADS),
        in_specs=[qk, qk, vo],
        out_specs=vo,
        out_shape=jax.ShapeDtypeStruct(v.shape, BF16),
        compiler_params=_params("parallel", "parallel"),
        name="mla_prompt",
    )(q, k, v)


def _mla_sample_kernel(q_ref, kp_ref, vp_ref, kn_ref, vn_ref, o_ref, *, past):
    n = q_ref.shape[1]
    q = q_ref[0]
    s_past = _nt_dot(q, kp_ref[0])
    s_new = _nt_dot(q, kn_ref[0])
    t_pos = past + lax.broadcasted_iota(jnp.int32, (n, n), 0)
    s_pos = past + lax.broadcasted_iota(jnp.int32, (n, n), 1)
    s_new = jnp.where((s_pos // CHUNK) <= (t_pos // CHUNK), s_new, NEG_INF)
    m = jnp.maximum(jnp.max(s_past, axis=-1, keepdims=True), jnp.max(s_new, axis=-1, keepdims=True))
    p_past = jnp.exp(s_past - m)
    p_new = jnp.exp(s_new - m)
    l = jnp.sum(p_past, axis=-1, keepdims=True) + jnp.sum(p_new, axis=-1, keepdims=True)
    acc = _dot(p_past.astype(BF16), vp_ref[0]) + _dot(p_new.astype(BF16), vn_ref[0])
    o_ref[0] = (acc / l).astype(o_ref.dtype)


def _mla_sample(q, k_past, v_past, k_new, v_new):
    b, n, _ = q.shape
    past = k_past.shape[1]
    spec = lambda rows, width: pl.BlockSpec((1, rows, width), lambda bi, h: (bi, 0, h))
    return pl.pallas_call(
        functools.partial(_mla_sample_kernel, past=past),
        grid=(b, HEADS),
        in_specs=[spec(n, QK_PAD), spec(past, QK_PAD), spec(past, HEAD_DIM), spec(n, QK_PAD), spec(n, HEAD_DIM)],
        out_specs=spec(n, HEAD_DIM),
        out_shape=jax.ShapeDtypeStruct(v_new.shape, BF16),
        compiler_params=_params("parallel", "parallel"),
        name="mla_sample",
    )(q, k_past, v_past, k_new, v_new)


def _merge_kernel(a1_ref, a2_ref, w1_ref, w2_ref, g1_ref, g2_ref, o_ref):
    y1 = _dot(a1_ref[...], w1_ref[...])
    y2 = _dot(a2_ref[...], w2_ref[...])
    o_ref[...] = (g1_ref[...].astype(F32) * y1 + g2_ref[...].astype(F32) * y2).astype(o_ref.dtype)


def _merge(sb_out, mla_out, gate, w, tm=1024, tn=1024):
    m, k = sb_out.shape
    d = w["sb_proj"].shape[1]
    tm, tn = _tile(m, tm), _tile(d, tn)
    nj = d // tn
    a_spec = pl.BlockSpec((tm, k), lambda i, j: (i, 0))
    w_spec = pl.BlockSpec((k, tn), lambda i, j: (0, j))
    return pl.pallas_call(
        _merge_kernel,
        grid=(m // tm, nj),
        in_specs=[a_spec, a_spec, w_spec, w_spec,
                  pl.BlockSpec((tm, tn), lambda i, j: (i, j)), pl.BlockSpec((tm, tn), lambda i, j: (i, j + nj))],
        out_specs=pl.BlockSpec((tm, tn), lambda i, j: (i, j)),
        out_shape=jax.ShapeDtypeStruct((m, d), BF16),
        compiler_params=_params("parallel", "arbitrary"),
        name="merge",
    )(sb_out, mla_out, w["sb_proj"], w["mla_proj"], gate, gate)


def _wo_kernel(a_ref, w_ref, x_ref, o_ref):
    o_ref[...] = x_ref[...] + _dot(a_ref[...], w_ref[...])


def _wo_residual(mix, x, w, tm=1024, tn=1024):
    m, d = x.shape
    tile = (_tile(m, tm), _tile(d, tn))
    ij = lambda i, j: (i, j)
    (out,) = _matmul_call(_wo_kernel, mix, w["o"], [(x, tile, ij)], [((m, d), F32, tile, ij)], tm, tn, "wo_residual")
    return out


def _ffn_kernel(x_ref, g_ref, wup_ref, wdn_ref, o_ref, h_ref):
    @pl.when(pl.program_id(1) == 0)
    def _():
        x = x_ref[...]
        h_ref[...] = _rms(x, g_ref[...]).astype(h_ref.dtype)
        o_ref[...] = x

    u = _dot(h_ref[...], wup_ref[...])
    u = jnp.square(jnp.maximum(u, 0.0)).astype(BF16)
    o_ref[...] += _dot(u, wdn_ref[...])


def _ffn(x, w, tm=1024, tf=512):
    m, d = x.shape
    f = w["up"].shape[1]
    tm, tf = _tile(m, tm), _tile(f, tf)
    return pl.pallas_call(
        _ffn_kernel,
        grid=(m // tm, f // tf),
        in_specs=[pl.BlockSpec((tm, d), lambda i, j: (i, 0)), pl.BlockSpec((1, d), lambda i, j: (0, 0)),
                  pl.BlockSpec((d, tf), lambda i, j: (0, j)), pl.BlockSpec((tf, d), lambda i, j: (j, 0))],
        out_specs=pl.BlockSpec((tm, d), lambda i, j: (i, 0)),
        out_shape=jax.ShapeDtypeStruct((m, d), F32),
        scratch_shapes=[pltpu.VMEM((tm, d), BF16)],
        compiler_params=_params("parallel", "arbitrary"),
        name="ffn",
    )(x, w["norm2_g"], w["up"], w["down"])


def _rope_table(pos):
    inv_freq = ROPE_THETA ** (-jnp.arange(HALF_ROPE, dtype=F32) / HALF_ROPE)
    ang = pos.astype(F32)[:, None] * inv_freq[None, :]
    cos, sin = jnp.cos(ang), jnp.sin(ang)
    return jnp.concatenate([cos, cos, -sin, sin], axis=1)


def _rope_cols(w):
    x1, x2 = w[..., :HALF_ROPE], w[..., HALF_ROPE:]
    return jnp.concatenate([x1, x2, x2, x1], axis=-1)


def _pad_gain(g):
    return jnp.concatenate([g, jnp.zeros((QK_PAD - QK_DIM,), g.dtype)]).reshape(1, QK_PAD)


def _layer_weights(l, norm1_g, w_in, q_norm_g, k_norm_g, kv_norm_g, w_uk, w_uv, w_sb_proj, w_mla_proj, w_o,
                   norm2_g, w_up, w_down):
    depth, d = w_in.shape[:2]
    width = HEADS * HEAD_DIM
    latent = w_uk.shape[1]
    wi = w_in[l]
    o_q, o_ckv = 3 * width, 3 * width + HEADS * QK_DIM
    o_kr, o_gate = o_ckv + latent, o_ckv + latent + ROPE_DIM
    wq = wi[:, o_q:o_ckv].reshape(d, HEADS, QK_DIM)
    wq = jnp.concatenate([wq[..., :HEAD_DIM], _rope_cols(wq[..., HEAD_DIM:])], axis=-1).reshape(d, HEADS * QK_PAD)
    bf = lambda a: a.astype(BF16)
    return {
        "depth": depth,
        "norm1_g": norm1_g[l], "norm2_g": norm2_g[l].reshape(1, d),
        "sb_q": bf(wi[:, :width]), "sb_k": bf(wi[:, width:2 * width]), "sb_v": bf(wi[:, 2 * width:o_q]),
        "mla_q": bf(wq),
        "ckv_kr": bf(jnp.concatenate([wi[:, o_ckv:o_kr], _rope_cols(wi[:, o_kr:o_gate])], axis=1)),
        "gate": bf(wi[:, o_gate:]),
        "q_norm_g": _pad_gain(q_norm_g[l]), "k_norm_g": _pad_gain(k_norm_g[l]),
        "kv_norm_g": kv_norm_g[l].reshape(1, latent),
        "uk": bf(w_uk[l]), "uv": bf(w_uv[l]),
        "sb_proj": bf(w_sb_proj[l]), "mla_proj": bf(w_mla_proj[l]), "o": bf(w_o[l]),
        "up": bf(w_up[l]), "down": bf(w_down[l]),
    }


def _merge_and_ffn(x, sb_out, mla_out, gate, w):
    mix = _merge(sb_out, mla_out, gate, w)
    return _ffn(_wo_residual(mix, x, w), w)


def kernel(x_prompt, x_sample, cache_sb_k, cache_sb_v, cache_mla_ckv, cache_mla_krope, norm1_g, w_in, q_norm_g, k_norm_g, kv_norm_g, w_uk, w_uv, w_sb_proj, w_mla_proj, w_o, norm2_g, w_up, w_down):
    b, s, d = x_prompt.shape
    bs, n, _ = x_sample.shape
    depth, _, past = cache_sb_k.shape[:3]
    width = HEADS * HEAD_DIM
    latent = cache_mla_ckv.shape[-1]

    tab_p = jnp.tile(_rope_table(jnp.arange(s, dtype=jnp.int32)), (b, 1))
    tab_s = jnp.tile(_rope_table(past + jnp.arange(n, dtype=jnp.int32)), (bs, 1))
    cache_k = cache_sb_k.reshape(depth, bs, past * HEADS, HEAD_DIM)
    cache_v = cache_sb_v.reshape(depth, bs, past * HEADS, HEAD_DIM)
    cache_c = cache_mla_ckv.reshape(depth, bs * past, latent)
    cache_r = cache_mla_krope.reshape(depth, bs * past, ROPE_DIM)

    xp = x_prompt.reshape(b * s, d)
    xs = x_sample.reshape(bs * n, d)
    new_p = new_s = None
    seq = lambda a: a.reshape(b, s, a.shape[1])
    dec = lambda a: a.reshape(bs, n, a.shape[1])
    for l in range(depth):
        w = _layer_weights(l, norm1_g, w_in, q_norm_g, k_norm_g, kv_norm_g, w_uk, w_uv, w_sb_proj, w_mla_proj,
                           w_o, norm2_g, w_up, w_down)
        p, new_p = _project(_rmsnorm(xp, w["norm1_g"]), tab_p, w, l, new_p)
        sb_out = _sb_prompt(seq(p["q"]), seq(p["kb"]), seq(p["vb"]))
        k_mla, v_mla = _kv_project(p["ckvb"][None], 0, new_p[3], l, w)
        mla_out = _mla_prompt(seq(p["qm"]), seq(k_mla), seq(v_mla))
        xp = _merge_and_ffn(xp, sb_out.reshape(b * s, width), mla_out.reshape(b * s, width), p["gate"], w)
        q, new_s = _project(_rmsnorm(xs, w["norm1_g"]), tab_s, w, l, new_s)
        sb_out = _sb_sample(dec(q["q"]), dec(q["kb"]), dec(q["vb"]), cache_k, cache_v, l)
        kp, vp = _kv_project(cache_c, l, cache_r, l, w)
        kn, vn = _kv_project(q["ckvb"][None], 0, new_s[3], l, w)
        mla_out = _mla_sample(dec(q["qm"]), kp.reshape(bs, past, -1), vp.reshape(bs, past, -1), dec(kn), dec(vn))
        xs = _merge_and_ffn(xs, sb_out.reshape(bs * n, width), mla_out.reshape(bs * n, width), q["gate"], w)

    def caches(new, bb, t):
        k32, v32, ckv, kr = new
        return (k32.reshape(depth, bb, t, HEADS, HEAD_DIM), v32.reshape(depth, bb, t, HEADS, HEAD_DIM),
                ckv.reshape(depth, bb, t, latent), kr.reshape(depth, bb, t, ROPE_DIM))

    return (xp.reshape(b, s, d), xs.reshape(bs, n, d), *caches(new_p, b, s), *caches(new_s, bs, n))
```

```python
import functools

import jax
import jax.numpy as jnp
from jax import lax
from jax.experimental import pallas as pl
from jax.experimental.pallas import tpu as pltpu

F32 = jnp.float32
BF16 = jnp.bfloat16

CHUNK = 64
HEADS = 8
HEAD_DIM = 128
ROPE_DIM = 64
HALF_ROPE = ROPE_DIM // 2
QK_DIM = HEAD_DIM + ROPE_DIM
QK_PAD = 2 * HEAD_DIM
ROPE_THETA = 10000.0
NORM_EPS = 1e-6
NEG_INF = -1e30
LOG2E = 1.4426950408889634
SB_SCALE = LOG2E * HEAD_DIM ** -0.5
MLA_SCALE = LOG2E * QK_DIM ** -0.5
SB_DONE = 110.0 * LOG2E

VMEM_LIMIT_BYTES = 56 * 1024 * 1024


def _params(*sem):
    return pltpu.CompilerParams(dimension_semantics=sem, vmem_limit_bytes=VMEM_LIMIT_BYTES)


def _tile(m, pref):
    t = min(m, pref)
    assert m % t == 0, (m, pref)
    return t


def _nt_dot(a, b):
    return lax.dot_general(a, b, (((1,), (1,)), ((), ())), preferred_element_type=F32)


def _tn_dot(a, b):
    return lax.dot_general(a, b, (((0,), (0,)), ((), ())), preferred_element_type=F32)


def _dot(a, b):
    return jnp.dot(a, b, preferred_element_type=F32)


def _rms(x, g):
    ms = jnp.mean(x * x, axis=-1, keepdims=True)
    return x * lax.rsqrt(ms + NORM_EPS) * g


def _rmsnorm_kernel(x_ref, g_ref, o_ref):
    o_ref[...] = _rms(x_ref[...], g_ref[...]).astype(o_ref.dtype)


def _rmsnorm(x, g, tm=512):
    m, d = x.shape
    tm = _tile(m, tm)
    return pl.pallas_call(
        _rmsnorm_kernel,
        grid=(m // tm,),
        in_specs=[pl.BlockSpec((tm, d), lambda i: (i, 0)), pl.BlockSpec((1, d), lambda i: (0, 0))],
        out_specs=pl.BlockSpec((tm, d), lambda i: (i, 0)),
        out_shape=jax.ShapeDtypeStruct((m, d), BF16),
        compiler_params=_params("parallel"),
        name="rmsnorm",
    )(x, g.reshape(1, d))


def _matmul_call(body, a, w, extras, outs, tm, tn, name, carried=()):
    m, k = a.shape
    n = w.shape[1]
    tm, tn = _tile(m, tm), _tile(n, tn)
    n_in = 2 + len(extras)
    in_specs = [pl.BlockSpec((tm, k), lambda i, j: (i, 0)), pl.BlockSpec((k, tn), lambda i, j: (0, j))]
    in_specs += [pl.BlockSpec(bs, im) for _, bs, im in extras]
    in_specs += [pl.BlockSpec(memory_space=pl.ANY)] * len(carried)

    def kernel_fn(*refs):
        body(*refs[:n_in], *refs[n_in + len(carried):])

    return pl.pallas_call(
        kernel_fn,
        grid=(m // tm, n // tn),
        in_specs=in_specs,
        out_specs=[pl.BlockSpec(bs, im) for _, _, bs, im in outs],
        out_shape=[jax.ShapeDtypeStruct(s, d) for s, d, _, _ in outs],
        input_output_aliases={n_in + t: oi for t, (_, oi) in enumerate(carried)},
        compiler_params=_params("parallel", "arbitrary"),
        name=name,
    )(a, w, *[e[0] for e in extras], *[c[0] for c in carried])


def _sb_q_kernel(a_ref, w_ref, o_ref):
    o_ref[...] = (_dot(a_ref[...], w_ref[...]) * SB_SCALE).astype(o_ref.dtype)


def _sb_kv_kernel(a_ref, w_ref, cache_ref, o_ref):
    acc = _dot(a_ref[...], w_ref[...])
    o_ref[...] = acc.astype(o_ref.dtype)
    tm = acc.shape[0]
    for h in range(HEADS):
        cache_ref[pl.ds(h, tm, stride=HEADS), :] = acc[:, h * HEAD_DIM:(h + 1) * HEAD_DIM]


def _gate_kernel(a_ref, w_ref, o_ref):
    o_ref[...] = jax.nn.sigmoid(_dot(a_ref[...], w_ref[...])).astype(o_ref.dtype)


def _rope_tile(x, tab):
    t = x * tab
    return t + pltpu.roll(t, ROPE_DIM, 1)


def _qmla_kernel(a_ref, w_ref, tab_ref, g_ref, o_ref):
    a = a_ref[...]
    tab = tab_ref[...]
    low = (lax.broadcasted_iota(jnp.int32, (1, HEAD_DIM), 1) < ROPE_DIM).astype(F32)
    g_nope, g_rope = g_ref[:, :HEAD_DIM] * MLA_SCALE, g_ref[:, HEAD_DIM:] * MLA_SCALE
    for h in range(HEADS):
        c0 = h * QK_PAD
        acc = _dot(a, w_ref[:, c0:c0 + QK_PAD])
        nope = acc[:, :HEAD_DIM]
        rope = _rope_tile(acc[:, HEAD_DIM:], tab) * low
        ss = jnp.sum(nope * nope, axis=-1, keepdims=True) + jnp.sum(rope * rope, axis=-1, keepdims=True)
        rs = lax.rsqrt(ss * (1.0 / QK_DIM) + NORM_EPS)
        o_ref[:, c0:c0 + HEAD_DIM] = (nope * rs * g_nope).astype(o_ref.dtype)
        o_ref[:, c0 + HEAD_DIM:c0 + QK_PAD] = (rope * rs * g_rope).astype(o_ref.dtype)


def _ckv_kernel(a_ref, w_ref, tab_ref, g_ref, ckv_ref, kr_ref, ckvb_ref):
    a = a_ref[...]
    latent = ckv_ref.shape[1]
    kr_ref[...] = _rope_tile(_dot(a, w_ref[:, latent:]), tab_ref[...])[:, :ROPE_DIM]
    cn = _rms(_dot(a, w_ref[:, :latent]), g_ref[...])
    ckv_ref[...] = cn
    ckvb_ref[...] = cn.astype(ckvb_ref.dtype)


def _project(h, tab, w, layer, new_cache, tm=1024):
    m, d = h.shape
    tm = _tile(m, tm)
    width = HEADS * HEAD_DIM
    depth = w["depth"]
    latent = w["kv_norm_g"].shape[1]
    row = lambda i, j: (i, 0)
    lrow = lambda i, j: (layer, i, 0)
    full = lambda shape, dtype: (shape, dtype, (tm, shape[1]), row)
    carry = lambda idx, out: [] if new_cache is None else [(new_cache[idx], out)]

    (q,) = _matmul_call(_sb_q_kernel, h, w["sb_q"], [], [full((m, width), BF16)], tm, width, "proj_sb_q")
    cache_out = ((depth, m * HEADS, HEAD_DIM), F32, (None, tm * HEADS, HEAD_DIM), lrow)
    k32, kb = _matmul_call(_sb_kv_kernel, h, w["sb_k"], [], [cache_out, full((m, width), BF16)],
                           tm, width, "proj_sb_k", carry(0, 0))
    v32, vb = _matmul_call(_sb_kv_kernel, h, w["sb_v"], [], [cache_out, full((m, width), BF16)],
                           tm, width, "proj_sb_v", carry(1, 0))
    tmq = _tile(m, 512)
    tab_spec = lambda t: (tab, (t, HEAD_DIM), row)
    gq = (w["q_norm_g"], (1, QK_PAD), lambda i, j: (0, 0))
    (qm,) = _matmul_call(_qmla_kernel, h, w["mla_q"], [tab_spec(tmq), gq],
                         [((m, HEADS * QK_PAD), BF16, (tmq, HEADS * QK_PAD), row)], tmq, HEADS * QK_PAD, "proj_mla_q")
    gkv = (w["kv_norm_g"], (1, latent), lambda i, j: (0, 0))
    ckv32, kr, ckvb = _matmul_call(
        _ckv_kernel, h, w["ckv_kr"], [tab_spec(tm), gkv],
        [((depth, m, latent), F32, (None, tm, latent), lrow), ((depth, m, ROPE_DIM), F32, (None, tm, ROPE_DIM), lrow),
         full((m, latent), BF16)], tm, latent + HEAD_DIM, "proj_ckv", carry(2, 0) + carry(3, 1))
    tng = 1024
    (gate,) = _matmul_call(_gate_kernel, h, w["gate"], [], [((m, 2 * d), BF16, (tm, tng), lambda i, j: (i, j))],
                           tm, tng, "proj_gate")
    return dict(q=q, kb=kb, vb=vb, qm=qm, ckvb=ckvb, gate=gate), (k32, v32, ckv32, kr)


def _kvproj_kernel(c_ref, kr_ref, wuk_ref, wuv_ref, g_ref, k_ref, v_ref):
    c = c_ref[...].astype(BF16)
    kn = _dot(c, wuk_ref[...])
    v_ref[...] = _dot(c, wuv_ref[...]).astype(v_ref.dtype)
    kr = kr_ref[...]
    ss_kr = jnp.sum(kr * kr, axis=-1, keepdims=True)
    kr_pad = jnp.concatenate([kr, jnp.zeros_like(kr)], axis=1)
    g_nope, g_rope = g_ref[:, :HEAD_DIM], g_ref[:, HEAD_DIM:]
    for h in range(HEADS):
        nope = kn[:, h * HEAD_DIM:(h + 1) * HEAD_DIM]
        ss = jnp.sum(nope * nope, axis=-1, keepdims=True) + ss_kr
        rs = lax.rsqrt(ss * (1.0 / QK_DIM) + NORM_EPS)
        c0 = h * QK_PAD
        k_ref[:, c0:c0 + HEAD_DIM] = (nope * rs * g_nope).astype(k_ref.dtype)
        k_ref[:, c0 + HEAD_DIM:c0 + QK_PAD] = (kr_pad * rs * g_rope).astype(k_ref.dtype)


def _kv_project(c, c_layer, kr, kr_layer, w, tm=512):
    _, rows, latent = c.shape
    tm = _tile(rows, tm)
    width = HEADS * HEAD_DIM
    const = lambda i: (0, 0)
    return pl.pallas_call(
        _kvproj_kernel,
        grid=(rows // tm,),
        in_specs=[pl.BlockSpec((None, tm, latent), lambda i: (c_layer, i, 0)),
                  pl.BlockSpec((None, tm, ROPE_DIM), lambda i: (kr_layer, i, 0)),
                  pl.BlockSpec((latent, width), const), pl.BlockSpec((latent, width), const),
                  pl.BlockSpec((1, QK_PAD), const)],
        out_specs=[pl.BlockSpec((tm, HEADS * QK_PAD), lambda i: (i, 0)), pl.BlockSpec((tm, width), lambda i: (i, 0))],
        out_shape=[jax.ShapeDtypeStruct((rows, HEADS * QK_PAD), BF16), jax.ShapeDtypeStruct((rows, width), BF16)],
        compiler_params=_params("parallel"),
        name="mla_kv_project",
    )(c, kr, w["uk"], w["uv"], w["k_norm_g"])


def _later_sum_matrix(n, doubled):
    rows = 2 * n if doubled else n
    r = lax.broadcasted_iota(jnp.int32, (rows, n), 0)
    c = lax.broadcasted_iota(jnp.int32, (rows, n), 1)
    if doubled:
        r = jnp.where(r >= n, r - n, r)
    return jnp.where(r > c, 1.0, 0.0).astype(BF16)


def _sb_block(q, k, v, tri, rest, acc, mask):
    z = _nt_dot(q, k)
    sp = jnp.maximum(z, 0.0) + jnp.log2(1.0 + jnp.exp2(-jnp.abs(z)))
    if mask is not None:
        sp = jnp.where(mask, sp, 0.0)
    hi = sp.astype(BF16)
    lo = (sp - hi.astype(F32)).astype(BF16)
    if tri.shape[0] == 2 * tri.shape[1]:
        later = _dot(jnp.concatenate([hi, lo], axis=1), tri)
    else:
        later = _dot(hi, tri) + _dot(lo, tri)
    a = jnp.exp2(((z - sp) - later) - rest)
    if mask is not None:
        a = jnp.where(mask, a, 0.0)
    acc = acc + _dot(a.astype(BF16), v)
    rest = rest + jnp.sum(sp, axis=-1, keepdims=True)
    return rest, acc


def _unfinished(rest):
    return (jnp.min(rest) < SB_DONE).astype(jnp.int32)


def _causal_mask(n):
    t_idx = lax.broadcasted_iota(jnp.int32, (n, n), 0)
    s_idx = lax.broadcasted_iota(jnp.int32, (n, n), 1)
    return s_idx < t_idx


def _sb_prompt_kernel(q_ref, k_ref, v_ref, o_ref, *, blk):
    s_len = q_ref.shape[1]
    tri = _later_sum_matrix(blk, blk % HEAD_DIM == 0)
    causal = _causal_mask(blk)

    def kv(j0):
        return k_ref[0, pl.ds(j0, blk), :], v_ref[0, pl.ds(j0, blk), :]

    def nearest(q0, has_prev):
        q = q_ref[0, pl.ds(q0, blk), :]
        rest, acc = _sb_block(q, *kv(q0), tri, jnp.zeros((blk, 1), F32), jnp.zeros((blk, HEAD_DIM), F32), causal)
        if has_prev:
            rest, acc = _sb_block(q, *kv(q0 - blk), tri, rest, acc, None)
        return q, rest, acc

    def further(q, j_first, rest, acc):
        def cond(c):
            return jnp.logical_and(c[0] >= 0, c[1] > 0)

        def body(c):
            j, _, rest, acc = c
            rest, acc = _sb_block(q, *kv(pl.multiple_of(j * blk, blk)), tri, rest, acc, None)
            return j - 1, _unfinished(rest), rest, acc

        return lax.while_loop(cond, body, (j_first, _unfinished(rest), rest, acc))[3]

    starts = [i * blk for i in range(s_len // blk)]
    states = [nearest(q0, q0 > 0) for q0 in starts]
    unfinished = functools.reduce(jnp.maximum, [_unfinished(rest) for _, rest, _ in states])
    accs = lax.cond(
        unfinished > 0,
        lambda: tuple(further(q, i - 2, rest, acc) for i, (q, rest, acc) in enumerate(states)),
        lambda: tuple(acc for _, _, acc in states))
    for q0, acc in zip(starts, accs):
        o_ref[0, pl.ds(q0, blk), :] = acc.astype(o_ref.dtype)


def _sb_prompt(q, k, v, blk=256):
    b, s, _ = q.shape
    blk = _tile(s, blk)
    spec = pl.BlockSpec((1, s, HEAD_DIM), lambda bi, h: (bi, 0, h))
    return pl.pallas_call(
        functools.partial(_sb_prompt_kernel, blk=blk),
        grid=(b, HEADS),
        in_specs=[spec, spec, spec],
        out_specs=spec,
        out_shape=jax.ShapeDtypeStruct(q.shape, BF16),
        compiler_params=_params("parallel", "parallel"),
        name="sb_prompt",
    )(q, k, v)


def _sb_sample_kernel(q_ref, kn_ref, vn_ref, kp_ref, vp_ref, o_ref, rest_ref, acc_ref, live_ref, *, blk):
    c = pl.program_id(1)
    n = q_ref.shape[1]
    chunk = kp_ref.shape[0] // HEADS
    cols = lambda h: slice(h * HEAD_DIM, (h + 1) * HEAD_DIM)

    @pl.when(c == 0)
    def _():
        tri = _later_sum_matrix(n, n % HEAD_DIM == 0)
        causal = _causal_mask(n)
        live = jnp.int32(0)
        for h in range(HEADS):
            rest, acc = _sb_block(q_ref[0, :, cols(h)], kn_ref[0, :, cols(h)], vn_ref[0, :, cols(h)], tri,
                                  jnp.zeros((n, 1), F32), jnp.zeros((n, HEAD_DIM), F32), causal)
            rest_ref[h] = rest
            acc_ref[h] = acc
            live = jnp.maximum(live, _unfinished(rest))
        live_ref[0] = live

    @pl.when(live_ref[0] > 0)
    def _():
        tri = _later_sum_matrix(blk, blk % HEAD_DIM == 0)
        live = jnp.int32(0)
        for h in range(HEADS):
            q = q_ref[0, :, cols(h)]
            rest, acc = rest_ref[h], acc_ref[h]
            for jb in reversed(range(chunk // blk)):
                rows = pl.ds(jb * blk * HEADS + h, blk, stride=HEADS)
                rest, acc = _sb_block(q, kp_ref[rows, :].astype(BF16), vp_ref[rows, :].astype(BF16), tri, rest, acc, None)
            rest_ref[h] = rest
            acc_ref[h] = acc
            live = jnp.maximum(live, _unfinished(rest))
        live_ref[0] = live

    @pl.when(c == pl.num_programs(1) - 1)
    def _():
        for h in range(HEADS):
            o_ref[0, :, cols(h)] = acc_ref[h].astype(o_ref.dtype)


def _sb_sample(q, k_new, v_new, cache_k, cache_v, layer, chunk=512, blk=256):
    b, n, width = q.shape
    past = cache_k.shape[2] // HEADS
    chunk = _tile(past, chunk)
    blk = _tile(chunk, blk)
    nc = past // chunk
    new = pl.BlockSpec((1, n, width), lambda bi, c: (bi, 0, 0))
    old = pl.BlockSpec((None, None, chunk * HEADS, HEAD_DIM), lambda bi, c: (layer, bi, nc - 1 - c, 0))
    return pl.pallas_call(
        functools.partial(_sb_sample_kernel, blk=blk),
        grid=(b, nc),
        in_specs=[new, new, new, old, old],
        out_specs=new,
        out_shape=jax.ShapeDtypeStruct(q.shape, BF16),
        scratch_shapes=[pltpu.VMEM((HEADS, n, 1), F32), pltpu.VMEM((HEADS, n, HEAD_DIM), F32),
                        pltpu.SMEM((1,), jnp.int32)],
        compiler_params=_params("parallel", "arbitrary"),
        name="sb_sample",
    )(q, k_new, v_new, cache_k, cache_v)


def _mla_advance(state, st, v_prev, mask):
    m, l, acc, p_prev = state
    pv = _tn_dot(v_prev, p_prev)
    if mask is not None:
        st = jnp.where(mask, st, NEG_INF)
    m_new = jnp.maximum(m, jnp.max(st, axis=0, keepdims=True))
    alpha = jnp.exp2(m - m_new)
    p = jnp.exp2(st - m_new)
    l = alpha * l + jnp.sum(p, axis=0, keepdims=True)
    return m_new, l, alpha * (acc + pv), p.astype(BF16)


def _mla_prompt_kernel(q_ref, k_ref, v_ref, o_ref, *, blk):
    s_len = q_ref.shape[1]
    pair = 2 * blk
    k_idx = lax.broadcasted_iota(jnp.int32, (blk, pair), 0)
    q_idx = lax.broadcasted_iota(jnp.int32, (blk, pair), 1)
    mask_first = (k_idx // CHUNK) <= (q_idx // CHUNK)
    mask_second = jnp.logical_and(q_idx >= blk, (k_idx // CHUNK) <= ((q_idx - blk) // CHUNK))
    init = (jnp.full((1, pair), NEG_INF, F32), jnp.zeros((1, pair), F32), jnp.zeros((HEAD_DIM, pair), F32),
            jnp.zeros((blk, pair), BF16))

    def keys(j):
        return k_ref[0, pl.ds(j * blk, blk), :]

    def values(j):
        return v_ref[0, pl.ds(j * blk, blk), :]

    for pi in range(s_len // pair):
        ja = 2 * pi
        q_pair = q_ref[0, pl.ds(ja * blk, pair), :]
        state = init
        for j in range(ja):
            state = _mla_advance(state, _nt_dot(keys(j), q_pair), values(max(j - 1, 0)), None)
        state = _mla_advance(state, _nt_dot(keys(ja), q_pair), values(max(ja - 1, 0)), mask_first)
        state = _mla_advance(state, _nt_dot(keys(ja + 1), q_pair), values(ja), mask_second)
        _, l, acc, p_last = state
        acc = acc + _tn_dot(values(ja + 1), p_last)
        o_ref[0, pl.ds(ja * blk, pair), :] = (acc / l).T.astype(o_ref.dtype)


def _mla_prompt(q, k, v, blk=256):
    b, s, _ = q.shape
    blk = _tile(s // 2, blk)
    assert blk % CHUNK == 0
    qk = pl.BlockSpec((1, s, QK_PAD), lambda bi, h: (bi, 0, h))
    vo = pl.BlockSpec((1, s, HEAD_DIM), lambda bi, h: (bi, 0, h))
    return pl.pallas_call(
        functools.partial(_mla_prompt_kernel, blk=blk),
        grid=(b, HEADS),
        in_specs=[qk, qk, vo],
        out_specs=vo,
        out_shape=jax.ShapeDtypeStruct(v.shape, BF16),
        compiler_params=_params("parallel", "parallel"),
        name="mla_prompt",
    )(q, k, v)


def _mla_sample_kernel(q_ref, kp_ref, vp_ref, kn_ref, vn_ref, o_ref, *, past):
    n = q_ref.shape[1]
    q = q_ref[0]
    s_past = _nt_dot(q, kp_ref[0])
    s_new = _nt_dot(q, kn_ref[0])
    t_pos = past + lax.broadcasted_iota(jnp.int32, (n, n), 0)
    s_pos = past + lax.broadcasted_iota(jnp.int32, (n, n), 1)
    s_new = jnp.where((s_pos // CHUNK) <= (t_pos // CHUNK), s_new, NEG_INF)
    m = jnp.maximum(jnp.max(s_past, axis=-1, keepdims=True), jnp.max(s_new, axis=-1, keepdims=True))
    p_past = jnp.exp2(s_past - m)
    p_new = jnp.exp2(s_new - m)
    l = jnp.sum(p_past, axis=-1, keepdims=True) + jnp.sum(p_new, axis=-1, keepdims=True)
    acc = _dot(p_past.astype(BF16), vp_ref[0]) + _dot(p_new.astype(BF16), vn_ref[0])
    o_ref[0] = (acc / l).astype(o_ref.dtype)


def _mla_sample(q, k_past, v_past, k_new, v_new):
    b, n, _ = q.shape
    past = k_past.shape[1]
    spec = lambda rows, width: pl.BlockSpec((1, rows, width), lambda bi, h: (bi, 0, h))
    return pl.pallas_call(
        functools.partial(_mla_sample_kernel, past=past),
        grid=(b, HEADS),
        in_specs=[spec(n, QK_PAD), spec(past, QK_PAD), spec(past, HEAD_DIM), spec(n, QK_PAD), spec(n, HEAD_DIM)],
        out_specs=spec(n, HEAD_DIM),
        out_shape=jax.ShapeDtypeStruct(v_new.shape, BF16),
        compiler_params=_params("parallel", "parallel"),
        name="mla_sample",
    )(q, k_past, v_past, k_new, v_new)


def _merge_kernel(a1_ref, a2_ref, g_ref, x_ref, w1_ref, w2_ref, wo_ref, o_ref):
    d = o_ref.shape[1]
    y1 = _dot(a1_ref[...], w1_ref[...])
    y2 = _dot(a2_ref[...], w2_ref[...])
    mix = g_ref[:, :d].astype(F32) * y1 + g_ref[:, d:].astype(F32) * y2
    o_ref[...] = x_ref[...] + _dot(mix.astype(BF16), wo_ref[...])


def _merge_residual(sb_out, mla_out, gate, x, w, tm=512):
    m, k = sb_out.shape
    d = x.shape[1]
    tm = _tile(m, tm)
    row = lambda width: pl.BlockSpec((tm, width), lambda i: (i, 0))
    resident = lambda shape: pl.BlockSpec(shape, lambda i: (0, 0), pipeline_mode=pl.Buffered(1))
    return pl.pallas_call(
        _merge_kernel,
        grid=(m // tm,),
        in_specs=[row(k), row(k), row(2 * d), row(d), resident((k, d)), resident((k, d)), resident((d, d))],
        out_specs=row(d),
        out_shape=jax.ShapeDtypeStruct((m, d), F32),
        compiler_params=_params("parallel"),
        name="merge_residual",
    )(sb_out, mla_out, gate, x, w["sb_proj"], w["mla_proj"], w["o"])


def _ffn_kernel(x_ref, g_ref, wup_ref, wdn_ref, o_ref, h_ref):
    @pl.when(pl.program_id(1) == 0)
    def _():
        x = x_ref[...]
        h_ref[...] = _rms(x, g_ref[...]).astype(h_ref.dtype)
        o_ref[...] = x

    u = _dot(h_ref[...], wup_ref[...])
    u = jnp.square(jnp.maximum(u, 0.0)).astype(BF16)
    o_ref[...] += _dot(u, wdn_ref[...])


def _ffn(x, w, tm=1024, tf=512):
    m, d = x.shape
    f = w["up"].shape[1]
    tm, tf = _tile(m, tm), _tile(f, tf)
    return pl.pallas_call(
        _ffn_kernel,
        grid=(m // tm, f // tf),
        in_specs=[pl.BlockSpec((tm, d), lambda i, j: (i, 0)), pl.BlockSpec((1, d), lambda i, j: (0, 0)),
                  pl.BlockSpec((d, tf), lambda i, j: (0, j)), pl.BlockSpec((tf, d), lambda i, j: (j, 0))],
        out_specs=pl.BlockSpec((tm, d), lambda i, j: (i, 0)),
        out_shape=jax.ShapeDtypeStruct((m, d), F32),
        scratch_shapes=[pltpu.VMEM((tm, d), BF16)],
        compiler_params=_params("parallel", "arbitrary"),
        name="ffn",
    )(x, w["norm2_g"], w["up"], w["down"])


def _rope_table(pos):
    inv_freq = ROPE_THETA ** (-jnp.arange(HALF_ROPE, dtype=F32) / HALF_ROPE)
    ang = pos.astype(F32)[:, None] * inv_freq[None, :]
    cos, sin = jnp.cos(ang), jnp.sin(ang)
    return jnp.concatenate([cos, cos, -sin, sin], axis=1)


def _rope_cols(w):
    x1, x2 = w[..., :HALF_ROPE], w[..., HALF_ROPE:]
    return jnp.concatenate([x1, x2, x2, x1], axis=-1)


def _pad_gain(g):
    return jnp.concatenate([g, jnp.zeros((QK_PAD - QK_DIM,), g.dtype)]).reshape(1, QK_PAD)


def _layer_weights(l, norm1_g, w_in, q_norm_g, k_norm_g, kv_norm_g, w_uk, w_uv, w_sb_proj, w_mla_proj, w_o,
                   norm2_g, w_up, w_down):
    depth, d = w_in.shape[:2]
    width = HEADS * HEAD_DIM
    latent = w_uk.shape[1]
    wi = w_in[l]
    o_q, o_ckv = 3 * width, 3 * width + HEADS * QK_DIM
    o_kr, o_gate = o_ckv + latent, o_ckv + latent + ROPE_DIM
    wq = wi[:, o_q:o_ckv].reshape(d, HEADS, QK_DIM)
    wq = jnp.concatenate([wq[..., :HEAD_DIM], _rope_cols(wq[..., HEAD_DIM:])], axis=-1).reshape(d, HEADS * QK_PAD)
    bf = lambda a: a.astype(BF16)
    return {
        "depth": depth,
        "norm1_g": norm1_g[l], "norm2_g": norm2_g[l].reshape(1, d),
        "sb_q": bf(wi[:, :width]), "sb_k": bf(wi[:, width:2 * width]), "sb_v": bf(wi[:, 2 * width:o_q]),
        "mla_q": bf(wq),
        "ckv_kr": bf(jnp.concatenate([wi[:, o_ckv:o_kr], _rope_cols(wi[:, o_kr:o_gate])], axis=1)),
        "gate": bf(wi[:, o_gate:]),
        "q_norm_g": _pad_gain(q_norm_g[l]), "k_norm_g": _pad_gain(k_norm_g[l]),
        "kv_norm_g": kv_norm_g[l].reshape(1, latent),
        "uk": bf(w_uk[l]), "uv": bf(w_uv[l]),
        "sb_proj": bf(w_sb_proj[l]), "mla_proj": bf(w_mla_proj[l]), "o": bf(w_o[l]),
        "up": bf(w_up[l]), "down": bf(w_down[l]),
    }


def _merge_and_ffn(x, sb_out, mla_out, gate, w):
    return _ffn(_merge_residual(sb_out, mla_out, gate, x, w), w)


def kernel(x_prompt, x_sample, cache_sb_k, cache_sb_v, cache_mla_ckv, cache_mla_krope, norm1_g, w_in, q_norm_g, k_norm_g, kv_norm_g, w_uk, w_uv, w_sb_proj, w_mla_proj, w_o, norm2_g, w_up, w_down):
    b, s, d = x_prompt.shape
    bs, n, _ = x_sample.shape
    depth, _, past = cache_sb_k.shape[:3]
    width = HEADS * HEAD_DIM
    latent = cache_mla_ckv.shape[-1]

    tab_p = jnp.tile(_rope_table(jnp.arange(s, dtype=jnp.int32)), (b, 1))
    tab_s = jnp.tile(_rope_table(past + jnp.arange(n, dtype=jnp.int32)), (bs, 1))
    cache_k = cache_sb_k.reshape(depth, bs, past * HEADS, HEAD_DIM)
    cache_v = cache_sb_v.reshape(depth, bs, past * HEADS, HEAD_DIM)
    cache_c = cache_mla_ckv.reshape(depth, bs * past, latent)
    cache_r = cache_mla_krope.reshape(depth, bs * past, ROPE_DIM)

    xp = x_prompt.reshape(b * s, d)
    xs = x_sample.reshape(bs * n, d)
    new_p = new_s = None
    seq = lambda a: a.reshape(b, s, a.shape[1])
    dec = lambda a: a.reshape(bs, n, a.shape[1])
    for l in range(depth):
        w = _layer_weights(l, norm1_g, w_in, q_norm_g, k_norm_g, kv_norm_g, w_uk, w_uv, w_sb_proj, w_mla_proj,
                           w_o, norm2_g, w_up, w_down)
        p, new_p = _project(_rmsnorm(xp, w["norm1_g"]), tab_p, w, l, new_p)
        sb_out = _sb_prompt(seq(p["q"]), seq(p["kb"]), seq(p["vb"]))
        k_mla, v_mla = _kv_project(p["ckvb"][None], 0, new_p[3], l, w)
        mla_out = _mla_prompt(seq(p["qm"]), seq(k_mla), seq(v_mla))
        xp = _merge_and_ffn(xp, sb_out.reshape(b * s, width), mla_out.reshape(b * s, width), p["gate"], w)
        q, new_s = _project(_rmsnorm(xs, w["norm1_g"]), tab_s, w, l, new_s)
        sb_out = _sb_sample(dec(q["q"]), dec(q["kb"]), dec(q["vb"]), cache_k, cache_v, l)
        kp, vp = _kv_project(cache_c, l, cache_r, l, w)
        kn, vn = _kv_project(q["ckvb"][None], 0, new_s[3], l, w)
        mla_out = _mla_sample(dec(q["qm"]), kp.reshape(bs, past, -1), vp.reshape(bs, past, -1), dec(kn), dec(vn))
        xs = _merge_and_ffn(xs, sb_out.reshape(bs * n, width), mla_out.reshape(bs * n, width), q["gate"], w)

    def caches(new, bb, t):
        k32, v32, ckv, kr = new
        return (k32.reshape(depth, bb, t, HEADS, HEAD_DIM), v32.reshape(depth, bb, t, HEADS, HEAD_DIM),
                ckv.reshape(depth, bb, t, latent), kr.reshape(depth, bb, t, ROPE_DIM))

    return (xp.reshape(b, s, d), xs.reshape(bs, n, d), *caches(new_p, b, s), *caches(new_s, bs, n))
```

```python
import functools

import jax
import jax.numpy as jnp
from jax import lax
from jax.experimental import pallas as pl
from jax.experimental.pallas import tpu as pltpu

F32 = jnp.float32
BF16 = jnp.bfloat16

CHUNK = 64
HEADS = 8
HEAD_DIM = 128
ROPE_DIM = 64
HALF_ROPE = ROPE_DIM // 2
QK_DIM = HEAD_DIM + ROPE_DIM
QK_PAD = 2 * HEAD_DIM
ROPE_THETA = 10000.0
NORM_EPS = 1e-6
NEG_INF = -1e30
LOG2E = 1.4426950408889634
SB_SCALE = LOG2E * HEAD_DIM ** -0.5
MLA_SCALE = LOG2E * QK_DIM ** -0.5
SB_DONE = 110.0 * LOG2E

VMEM_LIMIT_BYTES = 56 * 1024 * 1024


def _params(*sem):
    return pltpu.CompilerParams(dimension_semantics=sem, vmem_limit_bytes=VMEM_LIMIT_BYTES)


def _tile(m, pref):
    t = min(m, pref)
    assert m % t == 0, (m, pref)
    return t


def _nt_dot(a, b):
    return lax.dot_general(a, b, (((1,), (1,)), ((), ())), preferred_element_type=F32)


def _tn_dot(a, b):
    return lax.dot_general(a, b, (((0,), (0,)), ((), ())), preferred_element_type=F32)


def _dot(a, b):
    return jnp.dot(a, b, preferred_element_type=F32)


def _rms(x, g):
    ms = jnp.mean(x * x, axis=-1, keepdims=True)
    return x * lax.rsqrt(ms + NORM_EPS) * g


def _matmul_call(body, a, w, extras, outs, tm, tn, name, carried=()):
    m, k = a.shape
    n = w.shape[1]
    tm, tn = _tile(m, tm), _tile(n, tn)
    n_in = 2 + len(extras)
    in_specs = [pl.BlockSpec((tm, k), lambda i, j: (i, 0)), pl.BlockSpec((k, tn), lambda i, j: (0, j))]
    in_specs += [pl.BlockSpec(bs, im) for _, bs, im in extras]
    in_specs += [pl.BlockSpec(memory_space=pl.ANY)] * len(carried)

    def kernel_fn(*refs):
        body(*refs[:n_in], *refs[n_in + len(carried):])

    return pl.pallas_call(
        kernel_fn,
        grid=(m // tm, n // tn),
        in_specs=in_specs,
        out_specs=[pl.BlockSpec(bs, im) for _, _, bs, im in outs],
        out_shape=[jax.ShapeDtypeStruct(s, d) for s, d, _, _ in outs],
        input_output_aliases={n_in + t: oi for t, (_, oi) in enumerate(carried)},
        compiler_params=_params("parallel", "arbitrary"),
        name=name,
    )(a, w, *[e[0] for e in extras], *[c[0] for c in carried])


def _sb_q_kernel(a_ref, w_ref, o_ref):
    o_ref[...] = (_dot(a_ref[...], w_ref[...]) * SB_SCALE).astype(o_ref.dtype)


def _layer_slab(ref, layer):
    if len(ref.shape) == 2:
        return ref
    for other in range(ref.shape[0]):
        if other != layer:
            ref[other] = jnp.zeros(ref.shape[1:], ref.dtype)
    return ref.at[layer]


def _sb_kv_kernel(a_ref, w_ref, cache_ref, o_ref, *, layer):
    acc = _dot(a_ref[...], w_ref[...])
    o_ref[...] = acc.astype(o_ref.dtype)
    tm = acc.shape[0]
    slab = _layer_slab(cache_ref, layer)
    for h in range(HEADS):
        slab[pl.ds(h, tm, stride=HEADS), :] = acc[:, h * HEAD_DIM:(h + 1) * HEAD_DIM]


def _norm_gate_kernel(x_ref, g_ref, w_ref, h_ref, o_ref):
    @pl.when(pl.program_id(1) == 0)
    def _():
        h_ref[...] = _rms(x_ref[...], g_ref[...]).astype(h_ref.dtype)

    o_ref[...] = jax.nn.sigmoid(_dot(h_ref[...], w_ref[...])).astype(o_ref.dtype)


def _norm_gate(x, g, w_gate, tm=1024, tn=1024):
    m, d = x.shape
    n = w_gate.shape[1]
    tm, tn = _tile(m, tm), _tile(n, tn)
    row = lambda i, j: (i, 0)
    return pl.pallas_call(
        _norm_gate_kernel,
        grid=(m // tm, n // tn),
        in_specs=[pl.BlockSpec((tm, d), row), pl.BlockSpec((1, d), lambda i, j: (0, 0)),
                  pl.BlockSpec((d, tn), lambda i, j: (0, j))],
        out_specs=[pl.BlockSpec((tm, d), row), pl.BlockSpec((tm, tn), lambda i, j: (i, j))],
        out_shape=[jax.ShapeDtypeStruct((m, d), BF16), jax.ShapeDtypeStruct((m, n), BF16)],
        compiler_params=_params("parallel", "arbitrary"),
        name="norm_gate",
    )(x, g.reshape(1, d), w_gate)


def _rope_tile(x, tab):
    t = x * tab
    return t + pltpu.roll(t, ROPE_DIM, 1)


def _qmla_kernel(a_ref, w_ref, tab_ref, g_ref, o_ref):
    a = a_ref[...]
    tab = tab_ref[...]
    low = (lax.broadcasted_iota(jnp.int32, (1, HEAD_DIM), 1) < ROPE_DIM).astype(F32)
    g_nope, g_rope = g_ref[:, :HEAD_DIM] * MLA_SCALE, g_ref[:, HEAD_DIM:] * MLA_SCALE
    for h in range(HEADS):
        c0 = h * QK_PAD
        acc = _dot(a, w_ref[:, c0:c0 + QK_PAD])
        nope = acc[:, :HEAD_DIM]
        rope = _rope_tile(acc[:, HEAD_DIM:], tab) * low
        ss = jnp.sum(nope * nope + rope * rope, axis=-1, keepdims=True)
        rs = lax.rsqrt(ss * (1.0 / QK_DIM) + NORM_EPS)
        o_ref[:, c0:c0 + HEAD_DIM] = (nope * rs * g_nope).astype(o_ref.dtype)
        o_ref[:, c0 + HEAD_DIM:c0 + QK_PAD] = (rope * rs * g_rope).astype(o_ref.dtype)


def _ckv_kernel(a_ref, w_ref, tab_ref, g_ref, ckv_ref, kr_ref, ckvb_ref, *, layer):
    a = a_ref[...]
    latent = ckvb_ref.shape[1]
    _layer_slab(kr_ref, layer)[...] = _rope_tile(_dot(a, w_ref[:, latent:]), tab_ref[...])[:, :ROPE_DIM]
    cn = _rms(_dot(a, w_ref[:, :latent]), g_ref[...])
    _layer_slab(ckv_ref, layer)[...] = cn
    ckvb_ref[...] = cn.astype(ckvb_ref.dtype)


def _project(x, tab, w, layer, new_cache, tm=1024):
    m, d = x.shape
    h, gate = _norm_gate(x, w["norm1_g"], w["gate"])
    tm = _tile(m, tm)
    width = HEADS * HEAD_DIM
    depth = w["depth"]
    latent = w["kv_norm_g"].shape[1]
    row = lambda i, j: (i, 0)
    full = lambda shape, dtype: (shape, dtype, (tm, shape[1]), row)
    carry = lambda idx, out: [] if new_cache is None else [(new_cache[idx], out)]

    def slab(rows, width, block_rows):
        if new_cache is None:
            return ((depth, rows, width), F32, (depth, block_rows, width), lambda i, j: (0, i, 0))
        return ((depth, rows, width), F32, (None, block_rows, width), lambda i, j: (layer, i, 0))

    (q,) = _matmul_call(_sb_q_kernel, h, w["sb_q"], [], [full((m, width), BF16)], tm, width, "proj_sb_q")
    cache_out = slab(m * HEADS, HEAD_DIM, tm * HEADS)
    sb_kv = functools.partial(_sb_kv_kernel, layer=layer)
    k32, kb = _matmul_call(sb_kv, h, w["sb_k"], [], [cache_out, full((m, width), BF16)],
                           tm, width, "proj_sb_k", carry(0, 0))
    v32, vb = _matmul_call(sb_kv, h, w["sb_v"], [], [cache_out, full((m, width), BF16)],
                           tm, width, "proj_sb_v", carry(1, 0))
    tmq = _tile(m, 512)
    tab_spec = lambda t: (tab, (t, HEAD_DIM), row)
    gq = (w["q_norm_g"], (1, QK_PAD), lambda i, j: (0, 0))
    (qm,) = _matmul_call(_qmla_kernel, h, w["mla_q"], [tab_spec(tmq), gq],
                         [((m, HEADS * QK_PAD), BF16, (tmq, HEADS * QK_PAD), row)], tmq, HEADS * QK_PAD, "proj_mla_q")
    gkv = (w["kv_norm_g"], (1, latent), lambda i, j: (0, 0))
    ckv32, kr, ckvb = _matmul_call(
        functools.partial(_ckv_kernel, layer=layer), h, w["ckv_kr"], [tab_spec(tm), gkv],
        [slab(m, latent, tm), slab(m, ROPE_DIM, tm), full((m, latent), BF16)],
        tm, latent + HEAD_DIM, "proj_ckv", carry(2, 0) + carry(3, 1))
    return dict(q=q, kb=kb, vb=vb, qm=qm, ckvb=ckvb, gate=gate), (k32, v32, ckv32, kr)


def _kvproj_kernel(c_ref, kr_ref, wuk_ref, wuv_ref, g_ref, k_ref, v_ref):
    c = c_ref[...].astype(BF16)
    kn = _dot(c, wuk_ref[...])
    v_ref[...] = _dot(c, wuv_ref[...]).astype(v_ref.dtype)
    kr = kr_ref[...]
    ss_kr = jnp.sum(kr * kr, axis=-1, keepdims=True)
    kr_pad = jnp.concatenate([kr, jnp.zeros_like(kr)], axis=1)
    g_nope, g_rope = g_ref[:, :HEAD_DIM], g_ref[:, HEAD_DIM:]
    for h in range(HEADS):
        nope = kn[:, h * HEAD_DIM:(h + 1) * HEAD_DIM]
        ss = jnp.sum(nope * nope, axis=-1, keepdims=True) + ss_kr
        rs = lax.rsqrt(ss * (1.0 / QK_DIM) + NORM_EPS)
        c0 = h * QK_PAD
        k_ref[:, c0:c0 + HEAD_DIM] = (nope * rs * g_nope).astype(k_ref.dtype)
        k_ref[:, c0 + HEAD_DIM:c0 + QK_PAD] = (kr_pad * rs * g_rope).astype(k_ref.dtype)


def _kv_project(c, c_layer, kr, kr_layer, w, tm=512):
    _, rows, latent = c.shape
    tm = _tile(rows, tm)
    width = HEADS * HEAD_DIM
    const = lambda i: (0, 0)
    return pl.pallas_call(
        _kvproj_kernel,
        grid=(rows // tm,),
        in_specs=[pl.BlockSpec((None, tm, latent), lambda i: (c_layer, i, 0)),
                  pl.BlockSpec((None, tm, ROPE_DIM), lambda i: (kr_layer, i, 0)),
                  pl.BlockSpec((latent, width), const), pl.BlockSpec((latent, width), const),
                  pl.BlockSpec((1, QK_PAD), const)],
        out_specs=[pl.BlockSpec((tm, HEADS * QK_PAD), lambda i: (i, 0)), pl.BlockSpec((tm, width), lambda i: (i, 0))],
        out_shape=[jax.ShapeDtypeStruct((rows, HEADS * QK_PAD), BF16), jax.ShapeDtypeStruct((rows, width), BF16)],
        compiler_params=_params("parallel"),
        name="mla_kv_project",
    )(c, kr, w["uk"], w["uv"], w["k_norm_g"])


def _later_sum_matrix(n, doubled):
    rows = 2 * n if doubled else n
    r = lax.broadcasted_iota(jnp.int32, (rows, n), 0)
    c = lax.broadcasted_iota(jnp.int32, (rows, n), 1)
    if doubled:
        r = jnp.where(r >= n, r - n, r)
    return jnp.where(r > c, 1.0, 0.0).astype(BF16)


def _sb_block(q, k, v, tri, rest, acc, mask):
    z = _nt_dot(q, k)
    sp = jnp.maximum(z, 0.0) + jnp.log2(1.0 + jnp.exp2(-jnp.abs(z)))
    if mask is not None:
        sp = jnp.where(mask, sp, 0.0)
    hi = sp.astype(BF16)
    lo = (sp - hi.astype(F32)).astype(BF16)
    if tri.shape[0] == 2 * tri.shape[1]:
        later = _dot(jnp.concatenate([hi, lo], axis=1), tri)
    else:
        later = _dot(hi, tri) + _dot(lo, tri)
    a = jnp.exp2(((z - sp) - later) - rest)
    if mask is not None:
        a = jnp.where(mask, a, 0.0)
    acc = acc + _dot(a.astype(BF16), v)
    rest = rest + jnp.sum(sp, axis=-1, keepdims=True)
    return rest, acc


def _unfinished(rest):
    return (jnp.min(rest) < SB_DONE).astype(jnp.int32)


def _causal_mask(n):
    t_idx = lax.broadcasted_iota(jnp.int32, (n, n), 0)
    s_idx = lax.broadcasted_iota(jnp.int32, (n, n), 1)
    return s_idx < t_idx


def _sb_prompt_kernel(q_ref, k_ref, v_ref, o_ref, *, blk):
    s_len = q_ref.shape[1]
    tri = _later_sum_matrix(blk, blk % HEAD_DIM == 0)
    causal = _causal_mask(blk)

    def kv(j0):
        return k_ref[0, pl.ds(j0, blk), :], v_ref[0, pl.ds(j0, blk), :]

    def nearest(q0, has_prev):
        q = q_ref[0, pl.ds(q0, blk), :]
        rest, acc = _sb_block(q, *kv(q0), tri, jnp.zeros((blk, 1), F32), jnp.zeros((blk, HEAD_DIM), F32), causal)
        if has_prev:
            rest, acc = _sb_block(q, *kv(q0 - blk), tri, rest, acc, None)
        return q, rest, acc

    def further(q, j_first, rest, acc):
        def cond(c):
            return jnp.logical_and(c[0] >= 0, c[1] > 0)

        def body(c):
            j, _, rest, acc = c
            rest, acc = _sb_block(q, *kv(pl.multiple_of(j * blk, blk)), tri, rest, acc, None)
            return j - 1, _unfinished(rest), rest, acc

        return lax.while_loop(cond, body, (j_first, _unfinished(rest), rest, acc))[3]

    starts = [i * blk for i in range(s_len // blk)]
    states = [nearest(q0, q0 > 0) for q0 in starts]
    unfinished = functools.reduce(jnp.maximum, [_unfinished(rest) for _, rest, _ in states])
    accs = lax.cond(
        unfinished > 0,
        lambda: tuple(further(q, i - 2, rest, acc) for i, (q, rest, acc) in enumerate(states)),
        lambda: tuple(acc for _, _, acc in states))
    for q0, acc in zip(starts, accs):
        o_ref[0, pl.ds(q0, blk), :] = acc.astype(o_ref.dtype)


def _sb_prompt(q, k, v, blk=256):
    b, s, _ = q.shape
    blk = _tile(s, blk)
    spec = pl.BlockSpec((1, s, HEAD_DIM), lambda bi, h: (bi, 0, h))
    return pl.pallas_call(
        functools.partial(_sb_prompt_kernel, blk=blk),
        grid=(b, HEADS),
        in_specs=[spec, spec, spec],
        out_specs=spec,
        out_shape=jax.ShapeDtypeStruct(q.shape, BF16),
        compiler_params=_params("parallel", "parallel"),
        name="sb_prompt",
    )(q, k, v)


def _sb_sample_kernel(q_ref, kn_ref, vn_ref, kc_ref, vc_ref, o_ref, kbuf, vbuf, sem, rest_ref, acc_ref,
                      *, layer, chunk, blk):
    b = pl.program_id(0)
    n = q_ref.shape[1]
    nc = kc_ref.shape[2] // (chunk * HEADS)
    cols = lambda h: slice(h * HEAD_DIM, (h + 1) * HEAD_DIM)

    def copies(c):
        rows = pl.ds(pl.multiple_of((nc - 1 - c) * (chunk * HEADS), chunk * HEADS), chunk * HEADS)
        return (pltpu.make_async_copy(kc_ref.at[layer, b, rows, :], kbuf, sem.at[0]),
                pltpu.make_async_copy(vc_ref.at[layer, b, rows, :], vbuf, sem.at[1]))

    for cp in copies(0):
        cp.start()

    tri_new = _later_sum_matrix(n, n % HEAD_DIM == 0)
    causal = _causal_mask(n)
    for h in range(HEADS):
        rest, acc = _sb_block(q_ref[0, :, cols(h)], kn_ref[0, :, cols(h)], vn_ref[0, :, cols(h)], tri_new,
                              jnp.zeros((n, 1), F32), jnp.zeros((n, HEAD_DIM), F32), causal)
        rest_ref[h] = rest
        acc_ref[h] = acc

    tri = _later_sum_matrix(blk, blk % HEAD_DIM == 0)

    def chunk_body(carry):
        c, _ = carry
        for cp in copies(c):
            cp.wait()
        live = jnp.int32(0)
        for h in range(HEADS):
            q = q_ref[0, :, cols(h)]
            rest, acc = rest_ref[h], acc_ref[h]
            for jb in reversed(range(chunk // blk)):
                rows = pl.ds(jb * blk * HEADS + h, blk, stride=HEADS)
                rest, acc = _sb_block(q, kbuf[rows, :].astype(BF16), vbuf[rows, :].astype(BF16), tri, rest, acc, None)
            rest_ref[h] = rest
            acc_ref[h] = acc
            live = jnp.maximum(live, _unfinished(rest))
        more = jnp.logical_and(c + 1 < nc, live > 0)

        @pl.when(more)
        def _():
            for cp in copies(c + 1):
                cp.start()

        return c + 1, more.astype(jnp.int32)

    lax.while_loop(lambda carry: carry[1] > 0, chunk_body, (jnp.int32(0), jnp.int32(1)))

    for h in range(HEADS):
        o_ref[0, :, cols(h)] = acc_ref[h].astype(o_ref.dtype)


def _sb_sample(q, k_new, v_new, cache_k, cache_v, layer, chunk=256, blk=256):
    b, n, width = q.shape
    past = cache_k.shape[2] // HEADS
    assert past > 0
    chunk = _tile(past, chunk)
    blk = _tile(chunk, blk)
    new = pl.BlockSpec((1, n, width), lambda bi: (bi, 0, 0))
    hbm = pl.BlockSpec(memory_space=pl.ANY)
    return pl.pallas_call(
        functools.partial(_sb_sample_kernel, layer=layer, chunk=chunk, blk=blk),
        grid=(b,),
        in_specs=[new, new, new, hbm, hbm],
        out_specs=new,
        out_shape=jax.ShapeDtypeStruct(q.shape, BF16),
        scratch_shapes=[pltpu.VMEM((chunk * HEADS, HEAD_DIM), F32), pltpu.VMEM((chunk * HEADS, HEAD_DIM), F32),
                        pltpu.SemaphoreType.DMA((2,)),
                        pltpu.VMEM((HEADS, n, 1), F32), pltpu.VMEM((HEADS, n, HEAD_DIM), F32)],
        compiler_params=_params("parallel"),
        name="sb_sample",
    )(q, k_new, v_new, cache_k, cache_v)


def _mla_advance(state, st, v_prev, mask):
    m, l, acc, p_prev = state
    pv = _tn_dot(v_prev, p_prev)
    if mask is not None:
        st = jnp.where(mask, st, NEG_INF)
    m_new = jnp.maximum(m, jnp.max(st, axis=0, keepdims=True))
    alpha = jnp.exp2(m - m_new)
    p = jnp.exp2(st - m_new)
    l = alpha * l + jnp.sum(p, axis=0, keepdims=True)
    return m_new, l, alpha * (acc + pv), p.astype(BF16)


def _mla_prompt_kernel(q_ref, k_ref, v_ref, o_ref, *, blk):
    s_len = q_ref.shape[1]
    k_idx = lax.broadcasted_iota(jnp.int32, (blk, blk), 0)
    q_idx = lax.broadcasted_iota(jnp.int32, (blk, blk), 1)
    chunk_mask = (k_idx // CHUNK) <= (q_idx // CHUNK)
    init = (jnp.full((1, blk), NEG_INF, F32), jnp.zeros((1, blk), F32), jnp.zeros((HEAD_DIM, blk), F32),
            jnp.zeros((blk, blk), BF16))

    def rows(j):
        return pl.ds(j * blk, blk)

    for i in range(s_len // blk):
        q = q_ref[0, rows(i), :]
        state = init
        for j in range(i + 1):
            st = _nt_dot(k_ref[0, rows(j), :], q)
            state = _mla_advance(state, st, v_ref[0, rows(max(j - 1, 0)), :], chunk_mask if j == i else None)
        _, l, acc, p_last = state
        acc = acc + _tn_dot(v_ref[0, rows(i), :], p_last)
        o_ref[0, rows(i), :] = (acc / l).T.astype(o_ref.dtype)


def _mla_prompt(q, k, v, blk=512):
    b, s, _ = q.shape
    blk = _tile(s, blk)
    assert blk % CHUNK == 0
    qk = pl.BlockSpec((1, s, QK_PAD), lambda bi, h: (bi, 0, h))
    vo = pl.BlockSpec((1, s, HEAD_DIM), lambda bi, h: (bi, 0, h))
    return pl.pallas_call(
        functools.partial(_mla_prompt_kernel, blk=blk),
        grid=(b, HEADS),
        in_specs=[qk, qk, vo],
        out_specs=vo,
        out_shape=jax.ShapeDtypeStruct(v.shape, BF16),
        compiler_params=_params("parallel", "parallel"),
        name="mla_prompt",
    )(q, k, v)


def _mla_sample_kernel(q_ref, kp_ref, vp_ref, kn_ref, vn_ref, o_ref, *, past):
    n = q_ref.shape[1]
    q = q_ref[0]
    s_past = _nt_dot(q, kp_ref[0])
    s_new = _nt_dot(q, kn_ref[0])
    t_pos = past + lax.broadcasted_iota(jnp.int32, (n, n), 0)
    s_pos = past + lax.broadcasted_iota(jnp.int32, (n, n), 1)
    s_new = jnp.where((s_pos // CHUNK) <= (t_pos // CHUNK), s_new, NEG_INF)
    m = jnp.maximum(jnp.max(s_past, axis=-1, keepdims=True), jnp.max(s_new, axis=-1, keepdims=True))
    p_past = jnp.exp2(s_past - m)
    p_new = jnp.exp2(s_new - m)
    l = jnp.sum(p_past, axis=-1, keepdims=True) + jnp.sum(p_new, axis=-1, keepdims=True)
    acc = _dot(p_past.astype(BF16), vp_ref[0]) + _dot(p_new.astype(BF16), vn_ref[0])
    o_ref[0] = (acc / l).astype(o_ref.dtype)


def _mla_sample(q, k_past, v_past, k_new, v_new):
    b, n, _ = q.shape
    past = k_past.shape[1]
    spec = lambda rows, width: pl.BlockSpec((1, rows, width), lambda bi, h: (bi, 0, h))
    return pl.pallas_call(
        functools.partial(_mla_sample_kernel, past=past),
        grid=(b, HEADS),
        in_specs=[spec(n, QK_PAD), spec(past, QK_PAD), spec(past, HEAD_DIM), spec(n, QK_PAD), spec(n, HEAD_DIM)],
        out_specs=spec(n, HEAD_DIM),
        out_shape=jax.ShapeDtypeStruct(v_new.shape, BF16),
        compiler_params=_params("parallel", "parallel"),
        name="mla_sample",
    )(q, k_past, v_past, k_new, v_new)


def _merge_kernel(a1_ref, a2_ref, g_ref, x_ref, w1_ref, w2_ref, wo_ref, o_ref):
    d = o_ref.shape[1]
    y1 = _dot(a1_ref[...], w1_ref[...])
    y2 = _dot(a2_ref[...], w2_ref[...])
    mix = g_ref[:, :d].astype(F32) * y1 + g_ref[:, d:].astype(F32) * y2
    o_ref[...] = x_ref[...] + _dot(mix.astype(BF16), wo_ref[...])


def _merge_residual(sb_out, mla_out, gate, x, w, tm=512):
    m, k = sb_out.shape
    d = x.shape[1]
    tm = _tile(m, tm)
    row = lambda width: pl.BlockSpec((tm, width), lambda i: (i, 0))
    resident = lambda shape: pl.BlockSpec(shape, lambda i: (0, 0), pipeline_mode=pl.Buffered(1))
    return pl.pallas_call(
        _merge_kernel,
        grid=(m // tm,),
        in_specs=[row(k), row(k), row(2 * d), row(d), resident((k, d)), resident((k, d)), resident((d, d))],
        out_specs=row(d),
        out_shape=jax.ShapeDtypeStruct((m, d), F32),
        compiler_params=_params("parallel"),
        name="merge_residual",
    )(sb_out, mla_out, gate, x, w["sb_proj"], w["mla_proj"], w["o"])


def _ffn_kernel(x_ref, g_ref, wup_ref, wdn_ref, o_ref, h_ref):
    @pl.when(pl.program_id(1) == 0)
    def _():
        x = x_ref[...]
        h_ref[...] = _rms(x, g_ref[...]).astype(h_ref.dtype)
        o_ref[...] = x

    u = _dot(h_ref[...], wup_ref[...])
    u = jnp.square(jnp.maximum(u, 0.0)).astype(BF16)
    o_ref[...] += _dot(u, wdn_ref[...])


def _ffn(x, w, tm=1024, tf=512):
    m, d = x.shape
    f = w["up"].shape[1]
    tm, tf = _tile(m, tm), _tile(f, tf)
    return pl.pallas_call(
        _ffn_kernel,
        grid=(m // tm, f // tf),
        in_specs=[pl.BlockSpec((tm, d), lambda i, j: (i, 0)), pl.BlockSpec((1, d), lambda i, j: (0, 0)),
                  pl.BlockSpec((d, tf), lambda i, j: (0, j)), pl.BlockSpec((tf, d), lambda i, j: (j, 0))],
        out_specs=pl.BlockSpec((tm, d), lambda i, j: (i, 0)),
        out_shape=jax.ShapeDtypeStruct((m, d), F32),
        scratch_shapes=[pltpu.VMEM((tm, d), BF16)],
        compiler_params=_params("parallel", "arbitrary"),
        name="ffn",
    )(x, w["norm2_g"], w["up"], w["down"])


def _rope_table(pos):
    inv_freq = ROPE_THETA ** (-jnp.arange(HALF_ROPE, dtype=F32) / HALF_ROPE)
    ang = pos.astype(F32)[:, None] * inv_freq[None, :]
    cos, sin = jnp.cos(ang), jnp.sin(ang)
    return jnp.concatenate([cos, cos, -sin, sin], axis=1)


def _rope_cols(w):
    x1, x2 = w[..., :HALF_ROPE], w[..., HALF_ROPE:]
    return jnp.concatenate([x1, x2, x2, x1], axis=-1)


def _pad_gain(g):
    return jnp.concatenate([g, jnp.zeros((QK_PAD - QK_DIM,), g.dtype)]).reshape(1, QK_PAD)


def _layer_weights(l, norm1_g, w_in, q_norm_g, k_norm_g, kv_norm_g, w_uk, w_uv, w_sb_proj, w_mla_proj, w_o,
                   norm2_g, w_up, w_down):
    depth, d = w_in.shape[:2]
    width = HEADS * HEAD_DIM
    latent = w_uk.shape[1]
    wi = w_in[l]
    o_q, o_ckv = 3 * width, 3 * width + HEADS * QK_DIM
    o_kr, o_gate = o_ckv + latent, o_ckv + latent + ROPE_DIM
    wq = wi[:, o_q:o_ckv].reshape(d, HEADS, QK_DIM)
    wq = jnp.concatenate([wq[..., :HEAD_DIM], _rope_cols(wq[..., HEAD_DIM:])], axis=-1).reshape(d, HEADS * QK_PAD)
    bf = lambda a: a.astype(BF16)
    return {
        "depth": depth,
        "norm1_g": norm1_g[l], "norm2_g": norm2_g[l].reshape(1, d),
        "sb_q": bf(wi[:, :width]), "sb_k": bf(wi[:, width:2 * width]), "sb_v": bf(wi[:, 2 * width:o_q]),
        "mla_q": bf(wq),
        "ckv_kr": bf(jnp.concatenate([wi[:, o_ckv:o_kr], _rope_cols(wi[:, o_kr:o_gate])], axis=1)),
        "gate": bf(wi[:, o_gate:]),
        "q_norm_g": _pad_gain(q_norm_g[l]), "k_norm_g": _pad_gain(k_norm_g[l]),
        "kv_norm_g": kv_norm_g[l].reshape(1, latent),
        "uk": bf(w_uk[l]), "uv": bf(w_uv[l]),
        "sb_proj": bf(w_sb_proj[l]), "mla_proj": bf(w_mla_proj[l]), "o": bf(w_o[l]),
        "up": bf(w_up[l]), "down": bf(w_down[l]),
    }


def _merge_and_ffn(x, sb_out, mla_out, gate, w):
    return _ffn(_merge_residual(sb_out, mla_out, gate, x, w), w)


def kernel(x_prompt, x_sample, cache_sb_k, cache_sb_v, cache_mla_ckv, cache_mla_krope, norm1_g, w_in, q_norm_g, k_norm_g, kv_norm_g, w_uk, w_uv, w_sb_proj, w_mla_proj, w_o, norm2_g, w_up, w_down):
    b, s, d = x_prompt.shape
    bs, n, _ = x_sample.shape
    depth, _, past = cache_sb_k.shape[:3]
    width = HEADS * HEAD_DIM
    latent = cache_mla_ckv.shape[-1]

    tab_p = jnp.tile(_rope_table(jnp.arange(s, dtype=jnp.int32)), (b, 1))
    tab_s = jnp.tile(_rope_table(past + jnp.arange(n, dtype=jnp.int32)), (bs, 1))
    cache_k = cache_sb_k.reshape(depth, bs, past * HEADS, HEAD_DIM)
    cache_v = cache_sb_v.reshape(depth, bs, past * HEADS, HEAD_DIM)
    cache_c = cache_mla_ckv.reshape(depth, bs * past, latent)
    cache_r = cache_mla_krope.reshape(depth, bs * past, ROPE_DIM)

    xp = x_prompt.reshape(b * s, d)
    xs = x_sample.reshape(bs * n, d)
    new_p = new_s = None
    seq = lambda a: a.reshape(b, s, a.shape[1])
    dec = lambda a: a.reshape(bs, n, a.shape[1])
    for l in range(depth):
        w = _layer_weights(l, norm1_g, w_in, q_norm_g, k_norm_g, kv_norm_g, w_uk, w_uv, w_sb_proj, w_mla_proj,
                           w_o, norm2_g, w_up, w_down)
        q, new_s = _project(xs, tab_s, w, l, new_s)
        sb_out = _sb_sample(dec(q["q"]), dec(q["kb"]), dec(q["vb"]), cache_k, cache_v, l)
        kp, vp = _kv_project(cache_c, l, cache_r, l, w)
        kn, vn = _kv_project(q["ckvb"][None], 0, new_s[3], l, w)
        mla_out = _mla_sample(dec(q["qm"]), kp.reshape(bs, past, -1), vp.reshape(bs, past, -1), dec(kn), dec(vn))
        xs = _merge_and_ffn(xs, sb_out.reshape(bs * n, width), mla_out.reshape(bs * n, width), q["gate"], w)
        p, new_p = _project(xp, tab_p, w, l, new_p)
        sb_out = _sb_prompt(seq(p["q"]), seq(p["kb"]), seq(p["vb"]))
        k_mla, v_mla = _kv_project(p["ckvb"][None], 0, new_p[3], l, w)
        mla_out = _mla_prompt(seq(p["qm"]), seq(k_mla), seq(v_mla))
        xp = _merge_and_ffn(xp, sb_out.reshape(b * s, width), mla_out.reshape(b * s, width), p["gate"], w)

    def caches(new, bb, t):
        k32, v32, ckv, kr = new
        return (k32.reshape(depth, bb, t, HEADS, HEAD_DIM), v32.reshape(depth, bb, t, HEADS, HEAD_DIM),
                ckv.reshape(depth, bb, t, latent), kr.reshape(depth, bb, t, ROPE_DIM))

    return (xp.reshape(b, s, d), xs.reshape(bs, n, d), *caches(new_p, b, s), *caches(new_s, bs, n))
```

```python
import functools

import jax
import jax.numpy as jnp
from jax import lax
from jax.experimental import pallas as pl
from jax.experimental.pallas import tpu as pltpu

F32 = jnp.float32
BF16 = jnp.bfloat16

CHUNK = 64
HEADS = 8
HEAD_DIM = 128
ROPE_DIM = 64
HALF_ROPE = ROPE_DIM // 2
QK_DIM = HEAD_DIM + ROPE_DIM
QK_PAD = 2 * HEAD_DIM
ROPE_THETA = 10000.0
NORM_EPS = 1e-6
NEG_INF = -1e30
LOG2E = 1.4426950408889634
SB_SCALE = LOG2E * HEAD_DIM ** -0.5
MLA_SCALE = LOG2E * QK_DIM ** -0.5
SB_DONE = 110.0 * LOG2E

VMEM_LIMIT_BYTES = 56 * 1024 * 1024


def _params(*sem):
    return pltpu.CompilerParams(dimension_semantics=sem, vmem_limit_bytes=VMEM_LIMIT_BYTES)


def _tile(m, pref):
    t = min(m, pref)
    assert m % t == 0, (m, pref)
    return t


def _nt_dot(a, b):
    return lax.dot_general(a, b, (((1,), (1,)), ((), ())), preferred_element_type=F32)


def _tn_dot(a, b):
    return lax.dot_general(a, b, (((0,), (0,)), ((), ())), preferred_element_type=F32)


def _dot(a, b):
    return jnp.dot(a, b, preferred_element_type=F32)


def _rms(x, g):
    ms = jnp.mean(x * x, axis=-1, keepdims=True)
    return x * lax.rsqrt(ms + NORM_EPS) * g


def _matmul_call(body, a, w, extras, outs, tm, tn, name, carried=()):
    m, k = a.shape
    n = w.shape[1]
    tm, tn = _tile(m, tm), _tile(n, tn)
    n_in = 2 + len(extras)
    in_specs = [pl.BlockSpec((tm, k), lambda i, j: (i, 0)), pl.BlockSpec((k, tn), lambda i, j: (0, j))]
    in_specs += [pl.BlockSpec(bs, im) for _, bs, im in extras]
    in_specs += [pl.BlockSpec(memory_space=pl.ANY)] * len(carried)

    def kernel_fn(*refs):
        body(*refs[:n_in], *refs[n_in + len(carried):])

    return pl.pallas_call(
        kernel_fn,
        grid=(m // tm, n // tn),
        in_specs=in_specs,
        out_specs=[pl.BlockSpec(bs, im) for _, _, bs, im in outs],
        out_shape=[jax.ShapeDtypeStruct(s, d) for s, d, _, _ in outs],
        input_output_aliases={n_in + t: oi for t, (_, oi) in enumerate(carried)},
        compiler_params=_params("parallel", "arbitrary"),
        name=name,
    )(a, w, *[e[0] for e in extras], *[c[0] for c in carried])


def _sb_q_kernel(a_ref, w_ref, o_ref):
    o_ref[...] = (_dot(a_ref[...], w_ref[...]) * SB_SCALE).astype(o_ref.dtype)


def _layer_slab(ref, layer):
    if len(ref.shape) == 2:
        return ref
    for other in range(ref.shape[0]):
        if other != layer:
            ref[other] = jnp.zeros(ref.shape[1:], ref.dtype)
    return ref.at[layer]


def _sb_kv_kernel(a_ref, w_ref, cache_ref, o_ref, *, layer):
    acc = _dot(a_ref[...], w_ref[...])
    o_ref[...] = acc.astype(o_ref.dtype)
    tm = acc.shape[0]
    slab = _layer_slab(cache_ref, layer)
    for h in range(HEADS):
        slab[pl.ds(h, tm, stride=HEADS), :] = acc[:, h * HEAD_DIM:(h + 1) * HEAD_DIM]


def _norm_gate_kernel(x_ref, g_ref, w_ref, h_ref, o_ref):
    @pl.when(pl.program_id(1) == 0)
    def _():
        h_ref[...] = _rms(x_ref[...], g_ref[...]).astype(h_ref.dtype)

    o_ref[...] = jax.nn.sigmoid(_dot(h_ref[...], w_ref[...])).astype(o_ref.dtype)


def _norm_gate(x, g, w_gate, tm=1024, tn=1024):
    m, d = x.shape
    n = w_gate.shape[1]
    tm, tn = _tile(m, tm), _tile(n, tn)
    row = lambda i, j: (i, 0)
    return pl.pallas_call(
        _norm_gate_kernel,
        grid=(m // tm, n // tn),
        in_specs=[pl.BlockSpec((tm, d), row), pl.BlockSpec((1, d), lambda i, j: (0, 0)),
                  pl.BlockSpec((d, tn), lambda i, j: (0, j))],
        out_specs=[pl.BlockSpec((tm, d), row), pl.BlockSpec((tm, tn), lambda i, j: (i, j))],
        out_shape=[jax.ShapeDtypeStruct((m, d), BF16), jax.ShapeDtypeStruct((m, n), BF16)],
        compiler_params=_params("parallel", "arbitrary"),
        name="norm_gate",
    )(x, g.reshape(1, d), w_gate)


def _rope_tile(x, tab):
    t = x * tab
    return t + pltpu.roll(t, ROPE_DIM, 1)


def _qmla_kernel(a_ref, w_ref, tab_ref, g_ref, o_ref):
    a = a_ref[...]
    tab = tab_ref[...]
    low = (lax.broadcasted_iota(jnp.int32, (1, HEAD_DIM), 1) < ROPE_DIM).astype(F32)
    g_nope, g_rope = g_ref[:, :HEAD_DIM] * MLA_SCALE, g_ref[:, HEAD_DIM:] * MLA_SCALE
    for h in range(HEADS):
        c0 = h * QK_PAD
        acc = _dot(a, w_ref[:, c0:c0 + QK_PAD])
        nope = acc[:, :HEAD_DIM]
        rope = _rope_tile(acc[:, HEAD_DIM:], tab) * low
        ss = jnp.sum(nope * nope + rope * rope, axis=-1, keepdims=True)
        rs = lax.rsqrt(ss * (1.0 / QK_DIM) + NORM_EPS)
        o_ref[:, c0:c0 + HEAD_DIM] = (nope * rs * g_nope).astype(o_ref.dtype)
        o_ref[:, c0 + HEAD_DIM:c0 + QK_PAD] = (rope * rs * g_rope).astype(o_ref.dtype)


def _ckv_kernel(a_ref, w_ref, tab_ref, g_ref, ckv_ref, kr_ref, ckvb_ref, *, layer):
    a = a_ref[...]
    latent = ckvb_ref.shape[1]
    _layer_slab(kr_ref, layer)[...] = _rope_tile(_dot(a, w_ref[:, latent:]), tab_ref[...])[:, :ROPE_DIM]
    cn = _rms(_dot(a, w_ref[:, :latent]), g_ref[...])
    _layer_slab(ckv_ref, layer)[...] = cn
    ckvb_ref[...] = cn.astype(ckvb_ref.dtype)


def _project(x, tab, w, layer, new_cache, tm=1024):
    m, d = x.shape
    h, gate = _norm_gate(x, w["norm1_g"], w["gate"])
    tm = _tile(m, tm)
    width = HEADS * HEAD_DIM
    depth = w["depth"]
    latent = w["kv_norm_g"].shape[1]
    row = lambda i, j: (i, 0)
    full = lambda shape, dtype: (shape, dtype, (tm, shape[1]), row)
    carry = lambda idx, out: [] if new_cache is None else [(new_cache[idx], out)]

    def slab(rows, width, block_rows):
        if new_cache is None:
            return ((depth, rows, width), F32, (depth, block_rows, width), lambda i, j: (0, i, 0))
        return ((depth, rows, width), F32, (None, block_rows, width), lambda i, j: (layer, i, 0))

    (q,) = _matmul_call(_sb_q_kernel, h, w["sb_q"], [], [full((m, width), BF16)], tm, width, "proj_sb_q")
    cache_out = slab(m * HEADS, HEAD_DIM, tm * HEADS)
    sb_kv = functools.partial(_sb_kv_kernel, layer=layer)
    k32, kb = _matmul_call(sb_kv, h, w["sb_k"], [], [cache_out, full((m, width), BF16)],
                           tm, width, "proj_sb_k", carry(0, 0))
    v32, vb = _matmul_call(sb_kv, h, w["sb_v"], [], [cache_out, full((m, width), BF16)],
                           tm, width, "proj_sb_v", carry(1, 0))
    tmq = _tile(m, 512)
    tab_spec = lambda t: (tab, (t, HEAD_DIM), row)
    gq = (w["q_norm_g"], (1, QK_PAD), lambda i, j: (0, 0))
    (qm,) = _matmul_call(_qmla_kernel, h, w["mla_q"], [tab_spec(tmq), gq],
                         [((m, HEADS * QK_PAD), BF16, (tmq, HEADS * QK_PAD), row)], tmq, HEADS * QK_PAD, "proj_mla_q")
    gkv = (w["kv_norm_g"], (1, latent), lambda i, j: (0, 0))
    ckv32, kr, ckvb = _matmul_call(
        functools.partial(_ckv_kernel, layer=layer), h, w["ckv_kr"], [tab_spec(tm), gkv],
        [slab(m, latent, tm), slab(m, ROPE_DIM, tm), full((m, latent), BF16)],
        tm, latent + HEAD_DIM, "proj_ckv", carry(2, 0) + carry(3, 1))
    return dict(q=q, kb=kb, vb=vb, qm=qm, ckvb=ckvb, gate=gate), (k32, v32, ckv32, kr)


def _kvproj_kernel(c_ref, kr_ref, wuk_ref, wuv_ref, g_ref, k_ref, v_ref):
    c = c_ref[...].astype(BF16)
    kn = _dot(c, wuk_ref[...])
    v_ref[...] = _dot(c, wuv_ref[...]).astype(v_ref.dtype)
    kr = kr_ref[...]
    ss_kr = jnp.sum(kr * kr, axis=-1, keepdims=True)
    kr_pad = jnp.concatenate([kr, jnp.zeros_like(kr)], axis=1)
    g_nope, g_rope = g_ref[:, :HEAD_DIM], g_ref[:, HEAD_DIM:]
    for h in range(HEADS):
        nope = kn[:, h * HEAD_DIM:(h + 1) * HEAD_DIM]
        ss = jnp.sum(nope * nope, axis=-1, keepdims=True) + ss_kr
        rs = lax.rsqrt(ss * (1.0 / QK_DIM) + NORM_EPS)
        c0 = h * QK_PAD
        k_ref[:, c0:c0 + HEAD_DIM] = (nope * rs * g_nope).astype(k_ref.dtype)
        k_ref[:, c0 + HEAD_DIM:c0 + QK_PAD] = (kr_pad * rs * g_rope).astype(k_ref.dtype)


def _kv_project(c, c_layer, kr, kr_layer, w, tm=512):
    _, rows, latent = c.shape
    tm = _tile(rows, tm)
    width = HEADS * HEAD_DIM
    const = lambda i: (0, 0)
    return pl.pallas_call(
        _kvproj_kernel,
        grid=(rows // tm,),
        in_specs=[pl.BlockSpec((None, tm, latent), lambda i: (c_layer, i, 0)),
                  pl.BlockSpec((None, tm, ROPE_DIM), lambda i: (kr_layer, i, 0)),
                  pl.BlockSpec((latent, width), const), pl.BlockSpec((latent, width), const),
                  pl.BlockSpec((1, QK_PAD), const)],
        out_specs=[pl.BlockSpec((tm, HEADS * QK_PAD), lambda i: (i, 0)), pl.BlockSpec((tm, width), lambda i: (i, 0))],
        out_shape=[jax.ShapeDtypeStruct((rows, HEADS * QK_PAD), BF16), jax.ShapeDtypeStruct((rows, width), BF16)],
        compiler_params=_params("parallel"),
        name="mla_kv_project",
    )(c, kr, w["uk"], w["uv"], w["k_norm_g"])


def _later_sum_matrix(n, doubled):
    rows = 2 * n if doubled else n
    r = lax.broadcasted_iota(jnp.int32, (rows, n), 0)
    c = lax.broadcasted_iota(jnp.int32, (rows, n), 1)
    if doubled:
        r = jnp.where(r >= n, r - n, r)
    return jnp.where(r > c, 1.0, 0.0).astype(BF16)


def _sb_block(q, k, v, tri, rest, acc, mask):
    z = _nt_dot(q, k)
    sp = jnp.maximum(z, 0.0) + jnp.log2(1.0 + jnp.exp2(-jnp.abs(z)))
    if mask is not None:
        sp = jnp.where(mask, sp, 0.0)
    hi = sp.astype(BF16)
    lo = (sp - hi.astype(F32)).astype(BF16)
    if tri.shape[0] == 2 * tri.shape[1]:
        later = _dot(jnp.concatenate([hi, lo], axis=1), tri)
    else:
        later = _dot(hi, tri) + _dot(lo, tri)
    a = jnp.exp2(((z - sp) - later) - rest)
    if mask is not None:
        a = jnp.where(mask, a, 0.0)
    acc = acc + _dot(a.astype(BF16), v)
    rest = rest + jnp.sum(sp, axis=-1, keepdims=True)
    return rest, acc


def _unfinished(rest):
    return (jnp.min(rest) < SB_DONE).astype(jnp.int32)


def _causal_mask(n):
    t_idx = lax.broadcasted_iota(jnp.int32, (n, n), 0)
    s_idx = lax.broadcasted_iota(jnp.int32, (n, n), 1)
    return s_idx < t_idx


def _sb_prompt_kernel(q_ref, k_ref, v_ref, o_ref, *, blk):
    s_len = q_ref.shape[1]
    tri = _later_sum_matrix(blk, blk % HEAD_DIM == 0)
    causal = _causal_mask(blk)

    def kv(j0):
        return k_ref[0, pl.ds(j0, blk), :], v_ref[0, pl.ds(j0, blk), :]

    def nearest(q0, has_prev):
        q = q_ref[0, pl.ds(q0, blk), :]
        rest, acc = _sb_block(q, *kv(q0), tri, jnp.zeros((blk, 1), F32), jnp.zeros((blk, HEAD_DIM), F32), causal)
        if has_prev:
            rest, acc = _sb_block(q, *kv(q0 - blk), tri, rest, acc, None)
        return q, rest, acc

    def further(q, j_first, rest, acc):
        def cond(c):
            return jnp.logical_and(c[0] >= 0, c[1] > 0)

        def body(c):
            j, _, rest, acc = c
            rest, acc = _sb_block(q, *kv(pl.multiple_of(j * blk, blk)), tri, rest, acc, None)
            return j - 1, _unfinished(rest), rest, acc

        return lax.while_loop(cond, body, (j_first, _unfinished(rest), rest, acc))[3]

    starts = [i * blk for i in range(s_len // blk)]
    states = [nearest(q0, q0 > 0) for q0 in starts]
    unfinished = functools.reduce(jnp.maximum, [_unfinished(rest) for _, rest, _ in states])
    accs = lax.cond(
        unfinished > 0,
        lambda: tuple(further(q, i - 2, rest, acc) for i, (q, rest, acc) in enumerate(states)),
        lambda: tuple(acc for _, _, acc in states))
    for q0, acc in zip(starts, accs):
        o_ref[0, pl.ds(q0, blk), :] = acc.astype(o_ref.dtype)


def _sb_prompt(q, k, v, blk=256):
    b, s, _ = q.shape
    blk = _tile(s, blk)
    spec = pl.BlockSpec((1, s, HEAD_DIM), lambda bi, h: (bi, 0, h))
    return pl.pallas_call(
        functools.partial(_sb_prompt_kernel, blk=blk),
        grid=(b, HEADS),
        in_specs=[spec, spec, spec],
        out_specs=spec,
        out_shape=jax.ShapeDtypeStruct(q.shape, BF16),
        compiler_params=_params("parallel", "parallel"),
        name="sb_prompt",
    )(q, k, v)


def _sb_sample_kernel(q_ref, kn_ref, vn_ref, kc_ref, vc_ref, o_ref, kbuf, vbuf, sem, rest_ref, acc_ref,
                      *, layer, chunk, blk):
    b = pl.program_id(0)
    n = q_ref.shape[1]
    nc = kc_ref.shape[2] // (chunk * HEADS)
    cols = lambda h: slice(h * HEAD_DIM, (h + 1) * HEAD_DIM)

    def copies(c):
        rows = pl.ds(pl.multiple_of((nc - 1 - c) * (chunk * HEADS), chunk * HEADS), chunk * HEADS)
        return (pltpu.make_async_copy(kc_ref.at[layer, b, rows, :], kbuf, sem.at[0]),
                pltpu.make_async_copy(vc_ref.at[layer, b, rows, :], vbuf, sem.at[1]))

    for cp in copies(0):
        cp.start()

    tri_new = _later_sum_matrix(n, n % HEAD_DIM == 0)
    causal = _causal_mask(n)
    for h in range(HEADS):
        rest, acc = _sb_block(q_ref[0, :, cols(h)], kn_ref[0, :, cols(h)], vn_ref[0, :, cols(h)], tri_new,
                              jnp.zeros((n, 1), F32), jnp.zeros((n, HEAD_DIM), F32), causal)
        rest_ref[h] = rest
        acc_ref[h] = acc

    tri = _later_sum_matrix(blk, blk % HEAD_DIM == 0)

    def chunk_body(carry):
        c, _ = carry
        for cp in copies(c):
            cp.wait()
        live = jnp.int32(0)
        for h in range(HEADS):
            q = q_ref[0, :, cols(h)]
            rest, acc = rest_ref[h], acc_ref[h]
            for jb in reversed(range(chunk // blk)):
                rows = pl.ds(jb * blk * HEADS + h, blk, stride=HEADS)
                rest, acc = _sb_block(q, kbuf[rows, :].astype(BF16), vbuf[rows, :].astype(BF16), tri, rest, acc, None)
            rest_ref[h] = rest
            acc_ref[h] = acc
            live = jnp.maximum(live, _unfinished(rest))
        more = jnp.logical_and(c + 1 < nc, live > 0)

        @pl.when(more)
        def _():
            for cp in copies(c + 1):
                cp.start()

        return c + 1, more.astype(jnp.int32)

    lax.while_loop(lambda carry: carry[1] > 0, chunk_body, (jnp.int32(0), jnp.int32(1)))

    for h in range(HEADS):
        o_ref[0, :, cols(h)] = acc_ref[h].astype(o_ref.dtype)


def _sb_sample(q, k_new, v_new, cache_k, cache_v, layer, chunk=256, blk=256):
    b, n, width = q.shape
    past = cache_k.shape[2] // HEADS
    assert past > 0
    chunk = _tile(past, chunk)
    blk = _tile(chunk, blk)
    new = pl.BlockSpec((1, n, width), lambda bi: (bi, 0, 0))
    hbm = pl.BlockSpec(memory_space=pl.ANY)
    return pl.pallas_call(
        functools.partial(_sb_sample_kernel, layer=layer, chunk=chunk, blk=blk),
        grid=(b,),
        in_specs=[new, new, new, hbm, hbm],
        out_specs=new,
        out_shape=jax.ShapeDtypeStruct(q.shape, BF16),
        scratch_shapes=[pltpu.VMEM((chunk * HEADS, HEAD_DIM), F32), pltpu.VMEM((chunk * HEADS, HEAD_DIM), F32),
                        pltpu.SemaphoreType.DMA((2,)),
                        pltpu.VMEM((HEADS, n, 1), F32), pltpu.VMEM((HEADS, n, HEAD_DIM), F32)],
        compiler_params=_params("parallel"),
        name="sb_sample",
    )(q, k_new, v_new, cache_k, cache_v)


def _mla_advance(state, st, v_prev, mask):
    m, l, acc, p_prev = state
    pv = _tn_dot(v_prev, p_prev)
    if mask is not None:
        st = jnp.where(mask, st, NEG_INF)
    m_new = jnp.maximum(m, jnp.max(st, axis=0, keepdims=True))
    alpha = jnp.exp2(m - m_new)
    p = jnp.exp2(st - m_new)
    l = alpha * l + jnp.sum(p, axis=0, keepdims=True)
    return m_new, l, alpha * (acc + pv), p.astype(BF16)


def _mla_prompt_kernel(q_ref, k_ref, v_ref, o_ref, *, blk):
    s_len = q_ref.shape[1]
    k_idx = lax.broadcasted_iota(jnp.int32, (blk, blk), 0)
    q_idx = lax.broadcasted_iota(jnp.int32, (blk, blk), 1)
    chunk_mask = (k_idx // CHUNK) <= (q_idx // CHUNK)
    init = (jnp.full((1, blk), NEG_INF, F32), jnp.zeros((1, blk), F32), jnp.zeros((HEAD_DIM, blk), F32),
            jnp.zeros((blk, blk), BF16))

    def rows(j):
        return pl.ds(j * blk, blk)

    for i in range(s_len // blk):
        q = q_ref[0, rows(i), :]
        state = init
        for j in range(i + 1):
            st = _nt_dot(k_ref[0, rows(j), :], q)
            state = _mla_advance(state, st, v_ref[0, rows(max(j - 1, 0)), :], chunk_mask if j == i else None)
        _, l, acc, p_last = state
        acc = acc + _tn_dot(v_ref[0, rows(i), :], p_last)
        o_ref[0, rows(i), :] = (acc / l).T.astype(o_ref.dtype)


def _mla_prompt(q, k, v, blk=512):
    b, s, _ = q.shape
    blk = _tile(s, blk)
    assert blk % CHUNK == 0
    qk = pl.BlockSpec((1, s, QK_PAD), lambda bi, h: (bi, 0, h))
    vo = pl.BlockSpec((1, s, HEAD_DIM), lambda bi, h: (bi, 0, h))
    return pl.pallas_call(
        functools.partial(_mla_prompt_kernel, blk=blk),
        grid=(b, HEADS),
        in_specs=[qk, qk, vo],
        out_specs=vo,
        out_shape=jax.ShapeDtypeStruct(v.shape, BF16),
        compiler_params=_params("parallel", "parallel"),
        name="mla_prompt",
    )(q, k, v)


def _mla_sample_kernel(q_ref, cp_ref, rp_ref, cn_ref, rn_ref, wuk_ref, wukt_ref, wuv_ref, g_ref, o_ref,
                       *, past, chunk):
    n = q_ref.shape[1]
    latent = cp_ref.shape[1]
    g_nope, g_rope = g_ref[:, :HEAD_DIM], g_ref[:, HEAD_DIM:]
    rows = lambda h: slice(h * n, (h + 1) * n)

    qa, qr = [], []
    for h in range(HEADS):
        c0 = h * QK_PAD
        q_nope = (q_ref[0, :, c0:c0 + HEAD_DIM].astype(F32) * g_nope).astype(BF16)
        qa.append(_nt_dot(q_nope, wuk_ref[:, h * HEAD_DIM:(h + 1) * HEAD_DIM]).astype(BF16))
        q_rope = q_ref[0, :, c0 + HEAD_DIM:c0 + QK_PAD].astype(F32) * g_rope
        qr.append(q_rope[:, :ROPE_DIM].astype(BF16))
    lhs = jnp.concatenate(qa + [wukt_ref[...]], axis=0)
    qr = jnp.concatenate(qr, axis=0)

    def scores(c, kr):
        both = _nt_dot(lhs, c)
        s_rope = _nt_dot(qr, kr.astype(BF16))
        kr_t = kr.T
        ss_kr = jnp.sum(kr_t * kr_t, axis=0, keepdims=True)
        out = []
        for h in range(HEADS):
            kn_t = both[HEADS * n + h * HEAD_DIM:HEADS * n + (h + 1) * HEAD_DIM]
            ss = jnp.sum(kn_t * kn_t, axis=0, keepdims=True) + ss_kr
            rs = lax.rsqrt(ss * (1.0 / QK_DIM) + NORM_EPS)
            out.append((both[rows(h)] + s_rope[rows(h)]) * rs)
        return out

    c_new = cn_ref[...]
    k_pos = past + lax.broadcasted_iota(jnp.int32, (n, n), 1)
    q_pos = past + lax.broadcasted_iota(jnp.int32, (n, n), 0)
    visible = (k_pos // CHUNK) <= (q_pos // CHUNK)
    s = jnp.concatenate([jnp.where(visible, s_h, NEG_INF) for s_h in scores(c_new, rn_ref[...])], axis=0)
    m = jnp.max(s, axis=-1, keepdims=True)
    p = jnp.exp2(s - m)
    l = jnp.sum(p, axis=-1, keepdims=True)
    mix = _dot(p.astype(BF16), c_new)
    for j in range(past // chunk):
        ks = slice(j * chunk, (j + 1) * chunk)
        c = cp_ref[ks, :].astype(BF16)
        s = jnp.concatenate(scores(c, rp_ref[ks, :]), axis=0)
        m_new = jnp.maximum(m, jnp.max(s, axis=-1, keepdims=True))
        alpha = jnp.exp2(m - m_new)
        p = jnp.exp2(s - m_new)
        l = alpha * l + jnp.sum(p, axis=-1, keepdims=True)
        mix = alpha * mix + _dot(p.astype(BF16), c)
        m = m_new
    mix = (mix / l).astype(BF16)
    for h in range(HEADS):
        cols = slice(h * HEAD_DIM, (h + 1) * HEAD_DIM)
        o_ref[0, :, cols] = _dot(mix[rows(h)], wuv_ref[:, cols]).astype(o_ref.dtype)


def _mla_sample(q, cache_c, cache_r, layer, c_new, r_new, r_layer, w, chunk=1024):
    b, n, _ = q.shape
    latent = cache_c.shape[2]
    past = cache_c.shape[1] // b
    chunk = _tile(past, chunk)
    width = HEADS * HEAD_DIM
    const = lambda bi: (0, 0)
    return pl.pallas_call(
        functools.partial(_mla_sample_kernel, past=past, chunk=chunk),
        grid=(b,),
        in_specs=[pl.BlockSpec((1, n, HEADS * QK_PAD), lambda bi: (bi, 0, 0)),
                  pl.BlockSpec((None, past, latent), lambda bi: (layer, bi, 0)),
                  pl.BlockSpec((None, past, ROPE_DIM), lambda bi: (layer, bi, 0)),
                  pl.BlockSpec((n, latent), lambda bi: (bi, 0)),
                  pl.BlockSpec((None, n, ROPE_DIM), lambda bi: (r_layer, bi, 0)),
                  pl.BlockSpec((latent, width), const), pl.BlockSpec((width, latent), const),
                  pl.BlockSpec((latent, width), const), pl.BlockSpec((1, QK_PAD), const)],
        out_specs=pl.BlockSpec((1, n, width), lambda bi: (bi, 0, 0)),
        out_shape=jax.ShapeDtypeStruct((b, n, width), BF16),
        compiler_params=_params("parallel"),
        name="mla_sample",
    )(q, cache_c, cache_r, c_new, r_new, w["uk"], w["uk_t"], w["uv"], w["k_norm_g"])


def _merge_kernel(a1_ref, a2_ref, g_ref, x_ref, w1_ref, w2_ref, wo_ref, o_ref):
    d = o_ref.shape[1]
    y1 = _dot(a1_ref[...], w1_ref[...])
    y2 = _dot(a2_ref[...], w2_ref[...])
    mix = g_ref[:, :d].astype(F32) * y1 + g_ref[:, d:].astype(F32) * y2
    o_ref[...] = x_ref[...] + _dot(mix.astype(BF16), wo_ref[...])


def _merge_residual(sb_out, mla_out, gate, x, w, tm=512):
    m, k = sb_out.shape
    d = x.shape[1]
    tm = _tile(m, tm)
    row = lambda width: pl.BlockSpec((tm, width), lambda i: (i, 0))
    layer = w["layer"]
    resident = lambda shape: pl.BlockSpec((None,) + shape, lambda i: (layer, 0, 0), pipeline_mode=pl.Buffered(1))
    return pl.pallas_call(
        _merge_kernel,
        grid=(m // tm,),
        in_specs=[row(k), row(k), row(2 * d), row(d), resident((k, d)), resident((k, d)), resident((d, d))],
        out_specs=row(d),
        out_shape=jax.ShapeDtypeStruct((m, d), F32),
        compiler_params=_params("parallel"),
        name="merge_residual",
    )(sb_out, mla_out, gate, x, w["sb_proj"], w["mla_proj"], w["o"])


def _ffn_kernel(x_ref, g_ref, wup_ref, wdn_ref, o_ref, h_ref):
    @pl.when(pl.program_id(1) == 0)
    def _():
        x = x_ref[...]
        h_ref[...] = _rms(x, g_ref[...]).astype(h_ref.dtype)
        o_ref[...] = x

    u = _dot(h_ref[...], wup_ref[...])
    u = jnp.square(jnp.maximum(u, 0.0)).astype(BF16)
    o_ref[...] += _dot(u, wdn_ref[...])


def _ffn(x, w, tm=1024, tf=512):
    m, d = x.shape
    f = w["up"].shape[2]
    layer = w["layer"]
    tm, tf = _tile(m, tm), _tile(f, tf)
    return pl.pallas_call(
        _ffn_kernel,
        grid=(m // tm, f // tf),
        in_specs=[pl.BlockSpec((tm, d), lambda i, j: (i, 0)), pl.BlockSpec((1, d), lambda i, j: (0, 0)),
                  pl.BlockSpec((None, d, tf), lambda i, j: (layer, 0, j)),
                  pl.BlockSpec((None, tf, d), lambda i, j: (layer, j, 0))],
        out_specs=pl.BlockSpec((tm, d), lambda i, j: (i, 0)),
        out_shape=jax.ShapeDtypeStruct((m, d), F32),
        scratch_shapes=[pltpu.VMEM((tm, d), BF16)],
        compiler_params=_params("parallel", "arbitrary"),
        name="ffn",
    )(x, w["norm2_g"], w["up"], w["down"])


def _rope_table(pos):
    inv_freq = ROPE_THETA ** (-jnp.arange(HALF_ROPE, dtype=F32) / HALF_ROPE)
    ang = pos.astype(F32)[:, None] * inv_freq[None, :]
    cos, sin = jnp.cos(ang), jnp.sin(ang)
    return jnp.concatenate([cos, cos, -sin, sin], axis=1)


def _rope_cols(w):
    x1, x2 = w[..., :HALF_ROPE], w[..., HALF_ROPE:]
    return jnp.concatenate([x1, x2, x2, x1], axis=-1)


def _pad_gain(g):
    return jnp.concatenate([g, jnp.zeros((QK_PAD - QK_DIM,), g.dtype)]).reshape(1, QK_PAD)


def _layer_weights(l, stacks, norm1_g, w_in, q_norm_g, k_norm_g, kv_norm_g, w_uk, w_uv, norm2_g):
    depth, d = w_in.shape[:2]
    width = HEADS * HEAD_DIM
    latent = w_uk.shape[1]
    wi = w_in[l]
    o_q, o_ckv = 3 * width, 3 * width + HEADS * QK_DIM
    o_kr, o_gate = o_ckv + latent, o_ckv + latent + ROPE_DIM
    wq = wi[:, o_q:o_ckv].reshape(d, HEADS, QK_DIM)
    wq = jnp.concatenate([wq[..., :HEAD_DIM], _rope_cols(wq[..., HEAD_DIM:])], axis=-1).reshape(d, HEADS * QK_PAD)
    bf = lambda a: a.astype(BF16)
    return {
        "depth": depth, "layer": l, **stacks,
        "norm1_g": norm1_g[l], "norm2_g": norm2_g[l].reshape(1, d),
        "sb_q": bf(wi[:, :width]), "sb_k": bf(wi[:, width:2 * width]), "sb_v": bf(wi[:, 2 * width:o_q]),
        "mla_q": bf(wq),
        "ckv_kr": bf(jnp.concatenate([wi[:, o_ckv:o_kr], _rope_cols(wi[:, o_kr:o_gate])], axis=1)),
        "gate": bf(wi[:, o_gate:]),
        "q_norm_g": _pad_gain(q_norm_g[l]), "k_norm_g": _pad_gain(k_norm_g[l]),
        "kv_norm_g": kv_norm_g[l].reshape(1, latent),
        "uk": bf(w_uk[l]), "uk_t": bf(w_uk[l].T), "uv": bf(w_uv[l]),
    }


def _merge_and_ffn(x, sb_out, mla_out, gate, w):
    return _ffn(_merge_residual(sb_out, mla_out, gate, x, w), w)


def kernel(x_prompt, x_sample, cache_sb_k, cache_sb_v, cache_mla_ckv, cache_mla_krope, norm1_g, w_in, q_norm_g, k_norm_g, kv_norm_g, w_uk, w_uv, w_sb_proj, w_mla_proj, w_o, norm2_g, w_up, w_down):
    b, s, d = x_prompt.shape
    bs, n, _ = x_sample.shape
    depth, _, past = cache_sb_k.shape[:3]
    width = HEADS * HEAD_DIM
    latent = cache_mla_ckv.shape[-1]

    tab_p = jnp.tile(_rope_table(jnp.arange(s, dtype=jnp.int32)), (b, 1))
    tab_s = jnp.tile(_rope_table(past + jnp.arange(n, dtype=jnp.int32)), (bs, 1))
    cache_k = cache_sb_k.reshape(depth, bs, past * HEADS, HEAD_DIM)
    cache_v = cache_sb_v.reshape(depth, bs, past * HEADS, HEAD_DIM)
    cache_c = cache_mla_ckv.reshape(depth, bs * past, latent)
    cache_r = cache_mla_krope.reshape(depth, bs * past, ROPE_DIM)

    xp = x_prompt.reshape(b * s, d)
    xs = x_sample.reshape(bs * n, d)
    new_p = new_s = None
    stacks = {"sb_proj": w_sb_proj.astype(BF16), "mla_proj": w_mla_proj.astype(BF16), "o": w_o.astype(BF16),
              "up": w_up.astype(BF16), "down": w_down.astype(BF16)}
    seq = lambda a: a.reshape(b, s, a.shape[1])
    dec = lambda a: a.reshape(bs, n, a.shape[1])
    for l in range(depth):
        w = _layer_weights(l, stacks, norm1_g, w_in, q_norm_g, k_norm_g, kv_norm_g, w_uk, w_uv, norm2_g)
        q, new_s = _project(xs, tab_s, w, l, new_s)
        sb_out = _sb_sample(dec(q["q"]), dec(q["kb"]), dec(q["vb"]), cache_k, cache_v, l)
        mla_out = _mla_sample(dec(q["qm"]), cache_c, cache_r, l, q["ckvb"], new_s[3], l, w)
        xs = _merge_and_ffn(xs, sb_out.reshape(bs * n, width), mla_out.reshape(bs * n, width), q["gate"], w)
        p, new_p = _project(xp, tab_p, w, l, new_p)
        sb_out = _sb_prompt(seq(p["q"]), seq(p["kb"]), seq(p["vb"]))
        k_mla, v_mla = _kv_project(p["ckvb"][None], 0, new_p[3], l, w)
        mla_out = _mla_prompt(seq(p["qm"]), seq(k_mla), seq(v_mla))
        xp = _merge_and_ffn(xp, sb_out.reshape(b * s, width), mla_out.reshape(b * s, width), p["gate"], w)

    def caches(new, bb, t):
        k32, v32, ckv, kr = new
        return (k32.reshape(depth, bb, t, HEADS, HEAD_DIM), v32.reshape(depth, bb, t, HEADS, HEAD_DIM),
                ckv.reshape(depth, bb, t, latent), kr.reshape(depth, bb, t, ROPE_DIM))

    return (xp.reshape(b, s, d), xs.reshape(bs, n, d), *caches(new_p, b, s), *caches(new_s, bs, n))
```

```python
import functools

import jax
import jax.numpy as jnp
from jax import lax
from jax.experimental import pallas as pl
from jax.experimental.pallas import tpu as pltpu

F32 = jnp.float32
BF16 = jnp.bfloat16

CHUNK = 64
HEADS = 8
HEAD_DIM = 128
ROPE_DIM = 64
HALF_ROPE = ROPE_DIM // 2
QK_DIM = HEAD_DIM + ROPE_DIM
QK_PAD = 2 * HEAD_DIM
ROPE_THETA = 10000.0
NORM_EPS = 1e-6
NEG_INF = -1e30
LOG2E = 1.4426950408889634
SB_SCALE = LOG2E * HEAD_DIM ** -0.5
MLA_SCALE = LOG2E * QK_DIM ** -0.5
SB_DONE = 110.0 * LOG2E

VMEM_LIMIT_BYTES = 56 * 1024 * 1024


def _params(*sem):
    return pltpu.CompilerParams(dimension_semantics=sem, vmem_limit_bytes=VMEM_LIMIT_BYTES)


def _tile(m, pref):
    t = min(m, pref)
    assert m % t == 0, (m, pref)
    return t


def _nt_dot(a, b):
    return lax.dot_general(a, b, (((1,), (1,)), ((), ())), preferred_element_type=F32)


def _tn_dot(a, b):
    return lax.dot_general(a, b, (((0,), (0,)), ((), ())), preferred_element_type=F32)


def _dot(a, b):
    return jnp.dot(a, b, preferred_element_type=F32)


def _rms(x, g):
    ms = jnp.mean(x * x, axis=-1, keepdims=True)
    return x * lax.rsqrt(ms + NORM_EPS) * g


def _matmul_call(body, a, w, extras, outs, tm, tn, name, carried=(), resident=False):
    m, k = a.shape
    n = w.shape[1]
    tm, tn = _tile(m, tm), _tile(n, tn)
    n_in = 2 + len(extras)
    w_mode = {"pipeline_mode": pl.Buffered(1)} if resident else {}
    in_specs = [pl.BlockSpec((tm, k), lambda i, j: (i, 0)), pl.BlockSpec((k, tn), lambda i, j: (0, j), **w_mode)]
    in_specs += [pl.BlockSpec(bs, im) for _, bs, im in extras]
    in_specs += [pl.BlockSpec(memory_space=pl.ANY)] * len(carried)

    def kernel_fn(*refs):
        body(*refs[:n_in], *refs[n_in + len(carried):])

    return pl.pallas_call(
        kernel_fn,
        grid=(m // tm, n // tn),
        in_specs=in_specs,
        out_specs=[pl.BlockSpec(bs, im) for _, _, bs, im in outs],
        out_shape=[jax.ShapeDtypeStruct(s, d) for s, d, _, _ in outs],
        input_output_aliases={n_in + t: oi for t, (_, oi) in enumerate(carried)},
        compiler_params=_params("parallel", "arbitrary"),
        name=name,
    )(a, w, *[e[0] for e in extras], *[c[0] for c in carried])


def _layer_slab(ref, layer):
    if len(ref.shape) == 2:
        return ref
    for other in range(ref.shape[0]):
        if other != layer:
            ref[other] = jnp.zeros(ref.shape[1:], ref.dtype)
    return ref.at[layer]


def _sb_qkv_kernel(a_ref, w_ref, q_ref, kc_ref, kb_ref, vc_ref, vb_ref, *, layer):
    a = a_ref[...]
    tm = a.shape[0]
    width = q_ref.shape[1]
    q_ref[...] = (_dot(a, w_ref[:, :width]) * SB_SCALE).astype(q_ref.dtype)
    for part, (cache_ref, o_ref) in enumerate(((kc_ref, kb_ref), (vc_ref, vb_ref)), start=1):
        acc = _dot(a, w_ref[:, part * width:(part + 1) * width])
        o_ref[...] = acc.astype(o_ref.dtype)
        slab = _layer_slab(cache_ref, layer)
        for h in range(HEADS):
            slab[pl.ds(h, tm, stride=HEADS), :] = acc[:, h * HEAD_DIM:(h + 1) * HEAD_DIM]


def _norm_gate_kernel(x_ref, g_ref, w_ref, h_ref, o_ref, *, row_parts, col_parts):
    rp = x_ref.shape[0] // row_parts
    cp = w_ref.shape[1] // col_parts
    for r in range(row_parts):
        rows = slice(r * rp, (r + 1) * rp)
        h = _rms(x_ref[rows, :], g_ref[...]).astype(h_ref.dtype)
        h_ref[rows, :] = h
        for c in range(col_parts):
            cols = slice(c * cp, (c + 1) * cp)
            o_ref[rows, cols] = jax.nn.sigmoid(_dot(h, w_ref[:, cols])).astype(o_ref.dtype)


def _norm_gate(x, g, w_gate, tm=512, row_parts=2, col_parts=4):
    m, d = x.shape
    n = w_gate.shape[1]
    tm = _tile(m, tm)
    row = lambda i: (i, 0)
    return pl.pallas_call(
        functools.partial(_norm_gate_kernel, row_parts=row_parts, col_parts=col_parts),
        grid=(m // tm,),
        in_specs=[pl.BlockSpec((tm, d), row), pl.BlockSpec((1, d), lambda i: (0, 0)),
                  pl.BlockSpec((d, n), lambda i: (0, 0), pipeline_mode=pl.Buffered(1))],
        out_specs=[pl.BlockSpec((tm, d), row), pl.BlockSpec((tm, n), row)],
        out_shape=[jax.ShapeDtypeStruct((m, d), BF16), jax.ShapeDtypeStruct((m, n), BF16)],
        compiler_params=_params("parallel"),
        name="norm_gate",
    )(x, g.reshape(1, d), w_gate)


def _rope_tile(x, tab):
    t = x * tab
    return t + pltpu.roll(t, ROPE_DIM, 1)


def _qmla_kernel(a_ref, w_ref, tab_ref, g_ref, o_ref):
    a = a_ref[...]
    tab = tab_ref[...]
    low = (lax.broadcasted_iota(jnp.int32, (1, HEAD_DIM), 1) < ROPE_DIM).astype(F32)
    g_nope, g_rope = g_ref[:, :HEAD_DIM] * MLA_SCALE, g_ref[:, HEAD_DIM:] * MLA_SCALE
    for h in range(HEADS):
        c0 = h * QK_PAD
        acc = _dot(a, w_ref[:, c0:c0 + QK_PAD])
        nope = acc[:, :HEAD_DIM]
        rope = _rope_tile(acc[:, HEAD_DIM:], tab) * low
        ss = jnp.sum(nope * nope + rope * rope, axis=-1, keepdims=True)
        rs = lax.rsqrt(ss * (1.0 / QK_DIM) + NORM_EPS)
        o_ref[:, c0:c0 + HEAD_DIM] = (nope * rs * g_nope).astype(o_ref.dtype)
        o_ref[:, c0 + HEAD_DIM:c0 + QK_PAD] = (rope * rs * g_rope).astype(o_ref.dtype)


def _ckv_kernel(a_ref, w_ref, tab_ref, g_ref, ckv_ref, kr_ref, ckvb_ref, *, layer):
    a = a_ref[...]
    latent = ckvb_ref.shape[1]
    _layer_slab(kr_ref, layer)[...] = _rope_tile(_dot(a, w_ref[:, latent:]), tab_ref[...])[:, :ROPE_DIM]
    cn = _rms(_dot(a, w_ref[:, :latent]), g_ref[...])
    _layer_slab(ckv_ref, layer)[...] = cn
    ckvb_ref[...] = cn.astype(ckvb_ref.dtype)


def _project(x, tab, w, layer, new_cache, tm=1024):
    m, d = x.shape
    h, gate = _norm_gate(x, w["norm1_g"], w["gate"])
    tm = _tile(m, tm)
    width = HEADS * HEAD_DIM
    depth = w["depth"]
    latent = w["kv_norm_g"].shape[1]
    row = lambda i, j: (i, 0)
    full = lambda shape, dtype: (shape, dtype, (tm, shape[1]), row)
    carry = lambda idx, out: [] if new_cache is None else [(new_cache[idx], out)]

    def slab(rows, width, block_rows):
        if new_cache is None:
            return ((depth, rows, width), F32, (depth, block_rows, width), lambda i, j: (0, i, 0))
        return ((depth, rows, width), F32, (None, block_rows, width), lambda i, j: (layer, i, 0))

    tms = _tile(m, 512)
    cache_out = slab(m * HEADS, HEAD_DIM, tms * HEADS)
    half = ((m, width), BF16, (tms, width), row)
    q, k32, kb, v32, vb = _matmul_call(
        functools.partial(_sb_qkv_kernel, layer=layer), h, w["sb_qkv"], [], [half, cache_out, half, cache_out, half],
        tms, 3 * width, "proj_sb_qkv", carry(0, 1) + carry(1, 3), resident=True)
    tmq = _tile(m, 512)
    tab_spec = lambda t: (tab, (t, HEAD_DIM), row)
    gq = (w["q_norm_g"], (1, QK_PAD), lambda i, j: (0, 0))
    (qm,) = _matmul_call(_qmla_kernel, h, w["mla_q"], [tab_spec(tmq), gq],
                         [((m, HEADS * QK_PAD), BF16, (tmq, HEADS * QK_PAD), row)], tmq, HEADS * QK_PAD, "proj_mla_q")
    gkv = (w["kv_norm_g"], (1, latent), lambda i, j: (0, 0))
    ckv32, kr, ckvb = _matmul_call(
        functools.partial(_ckv_kernel, layer=layer), h, w["ckv_kr"], [tab_spec(tm), gkv],
        [slab(m, latent, tm), slab(m, ROPE_DIM, tm), full((m, latent), BF16)],
        tm, latent + HEAD_DIM, "proj_ckv", carry(2, 0) + carry(3, 1))
    return dict(q=q, kb=kb, vb=vb, qm=qm, ckvb=ckvb, gate=gate), (k32, v32, ckv32, kr)


def _kvproj_kernel(c_ref, kr_ref, wuk_ref, wuv_ref, g_ref, k_ref, v_ref):
    c = c_ref[...].astype(BF16)
    kn = _dot(c, wuk_ref[...])
    v_ref[...] = _dot(c, wuv_ref[...]).astype(v_ref.dtype)
    kr = kr_ref[...]
    ss_kr = jnp.sum(kr * kr, axis=-1, keepdims=True)
    kr_pad = jnp.concatenate([kr, jnp.zeros_like(kr)], axis=1)
    g_nope, g_rope = g_ref[:, :HEAD_DIM], g_ref[:, HEAD_DIM:]
    for h in range(HEADS):
        nope = kn[:, h * HEAD_DIM:(h + 1) * HEAD_DIM]
        ss = jnp.sum(nope * nope, axis=-1, keepdims=True) + ss_kr
        rs = lax.rsqrt(ss * (1.0 / QK_DIM) + NORM_EPS)
        c0 = h * QK_PAD
        k_ref[:, c0:c0 + HEAD_DIM] = (nope * rs * g_nope).astype(k_ref.dtype)
        k_ref[:, c0 + HEAD_DIM:c0 + QK_PAD] = (kr_pad * rs * g_rope).astype(k_ref.dtype)


def _kv_project(c, c_layer, kr, kr_layer, w, tm=512):
    _, rows, latent = c.shape
    tm = _tile(rows, tm)
    width = HEADS * HEAD_DIM
    const = lambda i: (0, 0)
    return pl.pallas_call(
        _kvproj_kernel,
        grid=(rows // tm,),
        in_specs=[pl.BlockSpec((None, tm, latent), lambda i: (c_layer, i, 0)),
                  pl.BlockSpec((None, tm, ROPE_DIM), lambda i: (kr_layer, i, 0)),
                  pl.BlockSpec((latent, width), const), pl.BlockSpec((latent, width), const),
                  pl.BlockSpec((1, QK_PAD), const)],
        out_specs=[pl.BlockSpec((tm, HEADS * QK_PAD), lambda i: (i, 0)), pl.BlockSpec((tm, width), lambda i: (i, 0))],
        out_shape=[jax.ShapeDtypeStruct((rows, HEADS * QK_PAD), BF16), jax.ShapeDtypeStruct((rows, width), BF16)],
        compiler_params=_params("parallel"),
        name="mla_kv_project",
    )(c, kr, w["uk"], w["uv"], w["k_norm_g"])


def _later_sum_matrix(n, doubled):
    rows = 2 * n if doubled else n
    r = lax.broadcasted_iota(jnp.int32, (rows, n), 0)
    c = lax.broadcasted_iota(jnp.int32, (rows, n), 1)
    if doubled:
        r = jnp.where(r >= n, r - n, r)
    return jnp.where(r > c, 1.0, 0.0).astype(BF16)


def _sb_block(q, k, v, tri, rest, acc, mask):
    z = _nt_dot(q, k)
    sp = jnp.maximum(z, 0.0) + jnp.log2(1.0 + jnp.exp2(-jnp.abs(z)))
    if mask is not None:
        sp = jnp.where(mask, sp, 0.0)
    hi = sp.astype(BF16)
    lo = (sp - hi.astype(F32)).astype(BF16)
    if tri.shape[0] == 2 * tri.shape[1]:
        later = _dot(jnp.concatenate([hi, lo], axis=1), tri)
    else:
        later = _dot(hi, tri) + _dot(lo, tri)
    a = jnp.exp2(((z - sp) - later) - rest)
    if mask is not None:
        a = jnp.where(mask, a, 0.0)
    acc = acc + _dot(a.astype(BF16), v)
    rest = rest + jnp.sum(sp, axis=-1, keepdims=True)
    return rest, acc


def _unfinished(rest):
    return (jnp.min(rest) < SB_DONE).astype(jnp.int32)


def _causal_mask(n):
    t_idx = lax.broadcasted_iota(jnp.int32, (n, n), 0)
    s_idx = lax.broadcasted_iota(jnp.int32, (n, n), 1)
    return s_idx < t_idx


def _sb_prompt_kernel(q_ref, k_ref, v_ref, o_ref, *, blk):
    s_len = q_ref.shape[1]
    tri = _later_sum_matrix(blk, blk % HEAD_DIM == 0)
    causal = _causal_mask(blk)

    def kv(j0):
        return k_ref[0, pl.ds(j0, blk), :], v_ref[0, pl.ds(j0, blk), :]

    def nearest(q0, has_prev):
        q = q_ref[0, pl.ds(q0, blk), :]
        rest, acc = _sb_block(q, *kv(q0), tri, jnp.zeros((blk, 1), F32), jnp.zeros((blk, HEAD_DIM), F32), causal)
        if has_prev:
            rest, acc = _sb_block(q, *kv(q0 - blk), tri, rest, acc, None)
        return q, rest, acc

    def further(q, j_first, rest, acc):
        def cond(c):
            return jnp.logical_and(c[0] >= 0, c[1] > 0)

        def body(c):
            j, _, rest, acc = c
            rest, acc = _sb_block(q, *kv(pl.multiple_of(j * blk, blk)), tri, rest, acc, None)
            return j - 1, _unfinished(rest), rest, acc

        return lax.while_loop(cond, body, (j_first, _unfinished(rest), rest, acc))[3]

    starts = [i * blk for i in range(s_len // blk)]
    states = [nearest(q0, q0 > 0) for q0 in starts]
    unfinished = functools.reduce(jnp.maximum, [_unfinished(rest) for _, rest, _ in states])
    accs = lax.cond(
        unfinished > 0,
        lambda: tuple(further(q, i - 2, rest, acc) for i, (q, rest, acc) in enumerate(states)),
        lambda: tuple(acc for _, _, acc in states))
    for q0, acc in zip(starts, accs):
        o_ref[0, pl.ds(q0, blk), :] = acc.astype(o_ref.dtype)


def _sb_prompt(q, k, v, blk=256):
    b, s, _ = q.shape
    blk = _tile(s, blk)
    spec = pl.BlockSpec((1, s, HEAD_DIM), lambda bi, h: (bi, 0, h))
    return pl.pallas_call(
        functools.partial(_sb_prompt_kernel, blk=blk),
        grid=(b, HEADS),
        in_specs=[spec, spec, spec],
        out_specs=spec,
        out_shape=jax.ShapeDtypeStruct(q.shape, BF16),
        compiler_params=_params("parallel", "parallel"),
        name="sb_prompt",
    )(q, k, v)


def _sb_sample_kernel(q_ref, kn_ref, vn_ref, kc_ref, vc_ref, o_ref, kbuf, vbuf, sem, rest_ref, acc_ref,
                      *, layer, chunk, blk):
    b = pl.program_id(0)
    n = q_ref.shape[1]
    nc = kc_ref.shape[2] // (chunk * HEADS)
    cols = lambda h: slice(h * HEAD_DIM, (h + 1) * HEAD_DIM)

    def copies(c):
        rows = pl.ds(pl.multiple_of((nc - 1 - c) * (chunk * HEADS), chunk * HEADS), chunk * HEADS)
        return (pltpu.make_async_copy(kc_ref.at[layer, b, rows, :], kbuf, sem.at[0]),
                pltpu.make_async_copy(vc_ref.at[layer, b, rows, :], vbuf, sem.at[1]))

    for cp in copies(0):
        cp.start()

    tri_new = _later_sum_matrix(n, n % HEAD_DIM == 0)
    causal = _causal_mask(n)
    for h in range(HEADS):
        rest, acc = _sb_block(q_ref[0, :, cols(h)], kn_ref[0, :, cols(h)], vn_ref[0, :, cols(h)], tri_new,
                              jnp.zeros((n, 1), F32), jnp.zeros((n, HEAD_DIM), F32), causal)
        rest_ref[h] = rest
        acc_ref[h] = acc

    tri = _later_sum_matrix(blk, blk % HEAD_DIM == 0)

    def chunk_body(carry):
        c, _ = carry
        for cp in copies(c):
            cp.wait()
        live = jnp.int32(0)
        for h in range(HEADS):
            q = q_ref[0, :, cols(h)]
            rest, acc = rest_ref[h], acc_ref[h]
            for jb in reversed(range(chunk // blk)):
                rows = pl.ds(jb * blk * HEADS + h, blk, stride=HEADS)
                rest, acc = _sb_block(q, kbuf[rows, :].astype(BF16), vbuf[rows, :].astype(BF16), tri, rest, acc, None)
            rest_ref[h] = rest
            acc_ref[h] = acc
            live = jnp.maximum(live, _unfinished(rest))
        more = jnp.logical_and(c + 1 < nc, live > 0)

        @pl.when(more)
        def _():
            for cp in copies(c + 1):
                cp.start()

        return c + 1, more.astype(jnp.int32)

    lax.while_loop(lambda carry: carry[1] > 0, chunk_body, (jnp.int32(0), jnp.int32(1)))

    for h in range(HEADS):
        o_ref[0, :, cols(h)] = acc_ref[h].astype(o_ref.dtype)


def _sb_sample(q, k_new, v_new, cache_k, cache_v, layer, chunk=256, blk=256):
    b, n, width = q.shape
    past = cache_k.shape[2] // HEADS
    assert past > 0
    chunk = _tile(past, chunk)
    blk = _tile(chunk, blk)
    new = pl.BlockSpec((1, n, width), lambda bi: (bi, 0, 0))
    hbm = pl.BlockSpec(memory_space=pl.ANY)
    return pl.pallas_call(
        functools.partial(_sb_sample_kernel, layer=layer, chunk=chunk, blk=blk),
        grid=(b,),
        in_specs=[new, new, new, hbm, hbm],
        out_specs=new,
        out_shape=jax.ShapeDtypeStruct(q.shape, BF16),
        scratch_shapes=[pltpu.VMEM((chunk * HEADS, HEAD_DIM), F32), pltpu.VMEM((chunk * HEADS, HEAD_DIM), F32),
                        pltpu.SemaphoreType.DMA((2,)),
                        pltpu.VMEM((HEADS, n, 1), F32), pltpu.VMEM((HEADS, n, HEAD_DIM), F32)],
        compiler_params=_params("parallel"),
        name="sb_sample",
    )(q, k_new, v_new, cache_k, cache_v)


def _mla_advance(state, st, v_prev, mask):
    m, l, acc, p_prev = state
    pv = _tn_dot(v_prev, p_prev)
    if mask is not None:
        st = jnp.where(mask, st, NEG_INF)
    m_new = jnp.maximum(m, jnp.max(st, axis=0, keepdims=True))
    alpha = jnp.exp2(m - m_new)
    p = jnp.exp2(st - m_new)
    l = alpha * l + jnp.sum(p, axis=0, keepdims=True)
    return m_new, l, alpha * (acc + pv), p.astype(BF16)


def _mla_prompt_kernel(q_ref, k_ref, v_ref, o_ref, *, blk):
    s_len = q_ref.shape[1]
    k_idx = lax.broadcasted_iota(jnp.int32, (blk, blk), 0)
    q_idx = lax.broadcasted_iota(jnp.int32, (blk, blk), 1)
    chunk_mask = (k_idx // CHUNK) <= (q_idx // CHUNK)
    init = (jnp.full((1, blk), NEG_INF, F32), jnp.zeros((1, blk), F32), jnp.zeros((HEAD_DIM, blk), F32),
            jnp.zeros((blk, blk), BF16))

    def rows(j):
        return pl.ds(j * blk, blk)

    for i in range(s_len // blk):
        q = q_ref[0, rows(i), :]
        state = init
        for j in range(i + 1):
            st = _nt_dot(k_ref[0, rows(j), :], q)
            state = _mla_advance(state, st, v_ref[0, rows(max(j - 1, 0)), :], chunk_mask if j == i else None)
        _, l, acc, p_last = state
        acc = acc + _tn_dot(v_ref[0, rows(i), :], p_last)
        o_ref[0, rows(i), :] = (acc / l).T.astype(o_ref.dtype)


def _mla_prompt(q, k, v, blk=512):
    b, s, _ = q.shape
    blk = _tile(s, blk)
    assert blk % CHUNK == 0
    qk = pl.BlockSpec((1, s, QK_PAD), lambda bi, h: (bi, 0, h))
    vo = pl.BlockSpec((1, s, HEAD_DIM), lambda bi, h: (bi, 0, h))
    return pl.pallas_call(
        functools.partial(_mla_prompt_kernel, blk=blk),
        grid=(b, HEADS),
        in_specs=[qk, qk, vo],
        out_specs=vo,
        out_shape=jax.ShapeDtypeStruct(v.shape, BF16),
        compiler_params=_params("parallel", "parallel"),
        name="mla_prompt",
    )(q, k, v)


def _mla_sample_kernel(q_ref, cp_ref, rp_ref, cn_ref, rn_ref, wuk_ref, wukt_ref, wuv_ref, g_ref, o_ref,
                       *, past, chunk):
    n = q_ref.shape[1]
    latent = cp_ref.shape[1]
    g_nope, g_rope = g_ref[:, :HEAD_DIM], g_ref[:, HEAD_DIM:]
    rows = lambda h: slice(h * n, (h + 1) * n)

    qa, qr = [], []
    for h in range(HEADS):
        c0 = h * QK_PAD
        q_nope = (q_ref[0, :, c0:c0 + HEAD_DIM].astype(F32) * g_nope).astype(BF16)
        qa.append(_nt_dot(q_nope, wuk_ref[:, h * HEAD_DIM:(h + 1) * HEAD_DIM]).astype(BF16))
        q_rope = q_ref[0, :, c0 + HEAD_DIM:c0 + QK_PAD].astype(F32) * g_rope
        qr.append(q_rope[:, :ROPE_DIM].astype(BF16))
    lhs = jnp.concatenate(qa + [wukt_ref[...]], axis=0)
    qr = jnp.concatenate(qr, axis=0)

    def scores(c, kr):
        both = _nt_dot(lhs, c)
        s_rope = _nt_dot(qr, kr.astype(BF16))
        kr_t = kr.T
        ss_kr = jnp.sum(kr_t * kr_t, axis=0, keepdims=True)
        out = []
        for h in range(HEADS):
            kn_t = both[HEADS * n + h * HEAD_DIM:HEADS * n + (h + 1) * HEAD_DIM]
            ss = jnp.sum(kn_t * kn_t, axis=0, keepdims=True) + ss_kr
            rs = lax.rsqrt(ss * (1.0 / QK_DIM) + NORM_EPS)
            out.append((both[rows(h)] + s_rope[rows(h)]) * rs)
        return out

    c_new = cn_ref[...]
    k_pos = past + lax.broadcasted_iota(jnp.int32, (n, n), 1)
    q_pos = past + lax.broadcasted_iota(jnp.int32, (n, n), 0)
    visible = (k_pos // CHUNK) <= (q_pos // CHUNK)
    s = jnp.concatenate([jnp.where(visible, s_h, NEG_INF) for s_h in scores(c_new, rn_ref[...])], axis=0)
    m = jnp.max(s, axis=-1, keepdims=True)
    p = jnp.exp2(s - m)
    l = jnp.sum(p, axis=-1, keepdims=True)
    mix = _dot(p.astype(BF16), c_new)
    for j in range(past // chunk):
        ks = slice(j * chunk, (j + 1) * chunk)
        c = cp_ref[ks, :].astype(BF16)
        s = jnp.concatenate(scores(c, rp_ref[ks, :]), axis=0)
        m_new = jnp.maximum(m, jnp.max(s, axis=-1, keepdims=True))
        alpha = jnp.exp2(m - m_new)
        p = jnp.exp2(s - m_new)
        l = alpha * l + jnp.sum(p, axis=-1, keepdims=True)
        mix = alpha * mix + _dot(p.astype(BF16), c)
        m = m_new
    mix = (mix / l).astype(BF16)
    for h in range(HEADS):
        cols = slice(h * HEAD_DIM, (h + 1) * HEAD_DIM)
        o_ref[0, :, cols] = _dot(mix[rows(h)], wuv_ref[:, cols]).astype(o_ref.dtype)


def _mla_sample(q, cache_c, cache_r, layer, c_new, r_new, r_layer, w, chunk=1024):
    b, n, _ = q.shape
    latent = cache_c.shape[2]
    past = cache_c.shape[1] // b
    chunk = _tile(past, chunk)
    width = HEADS * HEAD_DIM
    const = lambda bi: (0, 0)
    return pl.pallas_call(
        functools.partial(_mla_sample_kernel, past=past, chunk=chunk),
        grid=(b,),
        in_specs=[pl.BlockSpec((1, n, HEADS * QK_PAD), lambda bi: (bi, 0, 0)),
                  pl.BlockSpec((None, past, latent), lambda bi: (layer, bi, 0)),
                  pl.BlockSpec((None, past, ROPE_DIM), lambda bi: (layer, bi, 0)),
                  pl.BlockSpec((n, latent), lambda bi: (bi, 0)),
                  pl.BlockSpec((None, n, ROPE_DIM), lambda bi: (r_layer, bi, 0)),
                  pl.BlockSpec((latent, width), const), pl.BlockSpec((width, latent), const),
                  pl.BlockSpec((latent, width), const), pl.BlockSpec((1, QK_PAD), const)],
        out_specs=pl.BlockSpec((1, n, width), lambda bi: (bi, 0, 0)),
        out_shape=jax.ShapeDtypeStruct((b, n, width), BF16),
        compiler_params=_params("parallel"),
        name="mla_sample",
    )(q, cache_c, cache_r, c_new, r_new, w["uk"], w["uk_t"], w["uv"], w["k_norm_g"])


def _merge_kernel(a1_ref, a2_ref, g_ref, x_ref, w1_ref, w2_ref, wo_ref, o_ref):
    d = o_ref.shape[1]
    y1 = _dot(a1_ref[...], w1_ref[...])
    y2 = _dot(a2_ref[...], w2_ref[...])
    mix = g_ref[:, :d].astype(F32) * y1 + g_ref[:, d:].astype(F32) * y2
    o_ref[...] = x_ref[...] + _dot(mix.astype(BF16), wo_ref[...])


def _merge_residual(sb_out, mla_out, gate, x, w, tm=512):
    m, k = sb_out.shape
    d = x.shape[1]
    tm = _tile(m, tm)
    row = lambda width: pl.BlockSpec((tm, width), lambda i: (i, 0))
    layer = w["layer"]
    resident = lambda shape: pl.BlockSpec((None,) + shape, lambda i: (layer, 0, 0), pipeline_mode=pl.Buffered(1))
    return pl.pallas_call(
        _merge_kernel,
        grid=(m // tm,),
        in_specs=[row(k), row(k), row(2 * d), row(d), resident((k, d)), resident((k, d)), resident((d, d))],
        out_specs=row(d),
        out_shape=jax.ShapeDtypeStruct((m, d), F32),
        compiler_params=_params("parallel"),
        name="merge_residual",
    )(sb_out, mla_out, gate, x, w["sb_proj"], w["mla_proj"], w["o"])


def _ffn_kernel(x_ref, g_ref, wup_ref, wdn_ref, o_ref, h_ref):
    @pl.when(pl.program_id(1) == 0)
    def _():
        x = x_ref[...]
        h_ref[...] = _rms(x, g_ref[...]).astype(h_ref.dtype)
        o_ref[...] = x

    u = _dot(h_ref[...], wup_ref[...])
    u = jnp.square(jnp.maximum(u, 0.0)).astype(BF16)
    o_ref[...] += _dot(u, wdn_ref[...])


def _ffn(x, w, tm=1024, tf=512):
    m, d = x.shape
    f = w["up"].shape[2]
    layer = w["layer"]
    tm, tf = _tile(m, tm), _tile(f, tf)
    return pl.pallas_call(
        _ffn_kernel,
        grid=(m // tm, f // tf),
        in_specs=[pl.BlockSpec((tm, d), lambda i, j: (i, 0)), pl.BlockSpec((1, d), lambda i, j: (0, 0)),
                  pl.BlockSpec((None, d, tf), lambda i, j: (layer, 0, j)),
                  pl.BlockSpec((None, tf, d), lambda i, j: (layer, j, 0))],
        out_specs=pl.BlockSpec((tm, d), lambda i, j: (i, 0)),
        out_shape=jax.ShapeDtypeStruct((m, d), F32),
        scratch_shapes=[pltpu.VMEM((tm, d), BF16)],
        compiler_params=_params("parallel", "arbitrary"),
        name="ffn",
    )(x, w["norm2_g"], w["up"], w["down"])


def _rope_table(pos):
    inv_freq = ROPE_THETA ** (-jnp.arange(HALF_ROPE, dtype=F32) / HALF_ROPE)
    ang = pos.astype(F32)[:, None] * inv_freq[None, :]
    cos, sin = jnp.cos(ang), jnp.sin(ang)
    return jnp.concatenate([cos, cos, -sin, sin], axis=1)


def _rope_cols(w):
    x1, x2 = w[..., :HALF_ROPE], w[..., HALF_ROPE:]
    return jnp.concatenate([x1, x2, x2, x1], axis=-1)


def _pad_gain(g):
    return jnp.concatenate([g, jnp.zeros((QK_PAD - QK_DIM,), g.dtype)]).reshape(1, QK_PAD)


def _layer_weights(l, stacks, norm1_g, w_in, q_norm_g, k_norm_g, kv_norm_g, w_uk, w_uv, norm2_g):
    depth, d = w_in.shape[:2]
    width = HEADS * HEAD_DIM
    latent = w_uk.shape[1]
    wi = w_in[l]
    o_q, o_ckv = 3 * width, 3 * width + HEADS * QK_DIM
    o_kr, o_gate = o_ckv + latent, o_ckv + latent + ROPE_DIM
    wq = wi[:, o_q:o_ckv].reshape(d, HEADS, QK_DIM)
    wq = jnp.concatenate([wq[..., :HEAD_DIM], _rope_cols(wq[..., HEAD_DIM:])], axis=-1).reshape(d, HEADS * QK_PAD)
    bf = lambda a: a.astype(BF16)
    return {
        "depth": depth, "layer": l, **stacks,
        "norm1_g": norm1_g[l], "norm2_g": norm2_g[l].reshape(1, d),
        "sb_qkv": bf(wi[:, :o_q]),
        "mla_q": bf(wq),
        "ckv_kr": bf(jnp.concatenate([wi[:, o_ckv:o_kr], _rope_cols(wi[:, o_kr:o_gate])], axis=1)),
        "gate": bf(wi[:, o_gate:]),
        "q_norm_g": _pad_gain(q_norm_g[l]), "k_norm_g": _pad_gain(k_norm_g[l]),
        "kv_norm_g": kv_norm_g[l].reshape(1, latent),
        "uk": bf(w_uk[l]), "uk_t": bf(w_uk[l].T), "uv": bf(w_uv[l]),
    }


def _merge_and_ffn(x, sb_out, mla_out, gate, w):
    return _ffn(_merge_residual(sb_out, mla_out, gate, x, w), w)


def kernel(x_prompt, x_sample, cache_sb_k, cache_sb_v, cache_mla_ckv, cache_mla_krope, norm1_g, w_in, q_norm_g, k_norm_g, kv_norm_g, w_uk, w_uv, w_sb_proj, w_mla_proj, w_o, norm2_g, w_up, w_down):
    b, s, d = x_prompt.shape
    bs, n, _ = x_sample.shape
    depth, _, past = cache_sb_k.shape[:3]
    width = HEADS * HEAD_DIM
    latent = cache_mla_ckv.shape[-1]

    tab_p = jnp.tile(_rope_table(jnp.arange(s, dtype=jnp.int32)), (b, 1))
    tab_s = jnp.tile(_rope_table(past + jnp.arange(n, dtype=jnp.int32)), (bs, 1))
    cache_k = cache_sb_k.reshape(depth, bs, past * HEADS, HEAD_DIM)
    cache_v = cache_sb_v.reshape(depth, bs, past * HEADS, HEAD_DIM)
    cache_c = cache_mla_ckv.reshape(depth, bs * past, latent)
    cache_r = cache_mla_krope.reshape(depth, bs * past, ROPE_DIM)

    xp = x_prompt.reshape(b * s, d)
    xs = x_sample.reshape(bs * n, d)
    new_p = new_s = None
    stacks = {"sb_proj": w_sb_proj.astype(BF16), "mla_proj": w_mla_proj.astype(BF16), "o": w_o.astype(BF16),
              "up": w_up.astype(BF16), "down": w_down.astype(BF16)}
    seq = lambda a: a.reshape(b, s, a.shape[1])
    dec = lambda a: a.reshape(bs, n, a.shape[1])
    for l in range(depth):
        w = _layer_weights(l, stacks, norm1_g, w_in, q_norm_g, k_norm_g, kv_norm_g, w_uk, w_uv, norm2_g)
        q, new_s = _project(xs, tab_s, w, l, new_s)
        sb_out = _sb_sample(dec(q["q"]), dec(q["kb"]), dec(q["vb"]), cache_k, cache_v, l)
        mla_out = _mla_sample(dec(q["qm"]), cache_c, cache_r, l, q["ckvb"], new_s[3], l, w)
        xs = _merge_and_ffn(xs, sb_out.reshape(bs * n, width), mla_out.reshape(bs * n, width), q["gate"], w)
        p, new_p = _project(xp, tab_p, w, l, new_p)
        sb_out = _sb_prompt(seq(p["q"]), seq(p["kb"]), seq(p["vb"]))
        k_mla, v_mla = _kv_project(p["ckvb"][None], 0, new_p[3], l, w)
        mla_out = _mla_prompt(seq(p["qm"]), seq(k_mla), seq(v_mla))
        xp = _merge_and_ffn(xp, sb_out.reshape(b * s, width), mla_out.reshape(b * s, width), p["gate"], w)

    def caches(new, bb, t):
        k32, v32, ckv, kr = new
        return (k32.reshape(depth, bb, t, HEADS, HEAD_DIM), v32.reshape(depth, bb, t, HEADS, HEAD_DIM),
                ckv.reshape(depth, bb, t, latent), kr.reshape(depth, bb, t, ROPE_DIM))

    return (xp.reshape(b, s, d), xs.reshape(bs, n, d), *caches(new_p, b, s), *caches(new_s, bs, n))
```

```python
import functools

import jax
import jax.numpy as jnp
from jax import lax
from jax.experimental import pallas as pl
from jax.experimental.pallas import tpu as pltpu

F32 = jnp.float32
BF16 = jnp.bfloat16

CHUNK = 64
HEADS = 8
HEAD_DIM = 128
ROPE_DIM = 64
HALF_ROPE = ROPE_DIM // 2
QK_DIM = HEAD_DIM + ROPE_DIM
QK_PAD = 2 * HEAD_DIM
ROPE_THETA = 10000.0
NORM_EPS = 1e-6
NEG_INF = -1e30
LOG2E = 1.4426950408889634
SB_SCALE = LOG2E * HEAD_DIM ** -0.5
MLA_SCALE = LOG2E * QK_DIM ** -0.5
SB_DONE = 110.0 * LOG2E

MIB = 1024 * 1024
VMEM_LIMIT_BYTES = 56 * MIB
FFN_VMEM_LIMIT_BYTES = 62 * MIB


def _params(*sem, vmem_limit_bytes=VMEM_LIMIT_BYTES):
    return pltpu.CompilerParams(dimension_semantics=sem, vmem_limit_bytes=vmem_limit_bytes)


def _tile(m, pref):
    t = min(m, pref)
    assert m % t == 0, (m, pref)
    return t


def _nt_dot(a, b):
    return lax.dot_general(a, b, (((1,), (1,)), ((), ())), preferred_element_type=F32)


def _tn_dot(a, b):
    return lax.dot_general(a, b, (((0,), (0,)), ((), ())), preferred_element_type=F32)


def _dot(a, b):
    return jnp.dot(a, b, preferred_element_type=F32)


def _rms(x, g):
    ms = jnp.mean(x * x, axis=-1, keepdims=True)
    return x * lax.rsqrt(ms + NORM_EPS) * g


def _matmul_call(body, a, w, extras, outs, tm, tn, name, carried=(), resident=False):
    m, k = a.shape
    n = w.shape[1]
    tm, tn = _tile(m, tm), _tile(n, tn)
    n_in = 2 + len(extras)
    w_mode = {"pipeline_mode": pl.Buffered(1)} if resident else {}
    in_specs = [pl.BlockSpec((tm, k), lambda i, j: (i, 0)), pl.BlockSpec((k, tn), lambda i, j: (0, j), **w_mode)]
    in_specs += [pl.BlockSpec(bs, im) for _, bs, im in extras]
    in_specs += [pl.BlockSpec(memory_space=pl.ANY)] * len(carried)

    def kernel_fn(*refs):
        body(*refs[:n_in], *refs[n_in + len(carried):])

    return pl.pallas_call(
        kernel_fn,
        grid=(m // tm, n // tn),
        in_specs=in_specs,
        out_specs=[pl.BlockSpec(bs, im) for _, _, bs, im in outs],
        out_shape=[jax.ShapeDtypeStruct(s, d) for s, d, _, _ in outs],
        input_output_aliases={n_in + t: oi for t, (_, oi) in enumerate(carried)},
        compiler_params=_params("parallel", "arbitrary"),
        name=name,
    )(a, w, *[e[0] for e in extras], *[c[0] for c in carried])


def _layer_slab(ref, layer):
    if len(ref.shape) == 2:
        return ref
    for other in range(ref.shape[0]):
        if other != layer:
            ref[other] = jnp.zeros(ref.shape[1:], ref.dtype)
    return ref.at[layer]


def _sb_qkv_kernel(a_ref, w_ref, q_ref, kc_ref, kb_ref, vc_ref, vb_ref, *, layer):
    a = a_ref[...]
    tm = a.shape[0]
    width = q_ref.shape[1]
    q_ref[...] = (_dot(a, w_ref[:, :width]) * SB_SCALE).astype(q_ref.dtype)
    for part, (cache_ref, o_ref) in enumerate(((kc_ref, kb_ref), (vc_ref, vb_ref)), start=1):
        acc = _dot(a, w_ref[:, part * width:(part + 1) * width])
        o_ref[...] = acc.astype(o_ref.dtype)
        slab = _layer_slab(cache_ref, layer)
        for h in range(HEADS):
            slab[pl.ds(h, tm, stride=HEADS), :] = acc[:, h * HEAD_DIM:(h + 1) * HEAD_DIM]


def _norm_gate_kernel(x_ref, g_ref, w_ref, h_ref, o_ref, *, row_parts, col_parts):
    rp = x_ref.shape[0] // row_parts
    cp = w_ref.shape[1] // col_parts
    for r in range(row_parts):
        rows = slice(r * rp, (r + 1) * rp)
        h = _rms(x_ref[rows, :], g_ref[...]).astype(h_ref.dtype)
        h_ref[rows, :] = h
        for c in range(col_parts):
            cols = slice(c * cp, (c + 1) * cp)
            o_ref[rows, cols] = jax.nn.sigmoid(_dot(h, w_ref[:, cols])).astype(o_ref.dtype)


def _norm_gate(x, g, w_gate, tm=512, row_parts=2, col_parts=4):
    m, d = x.shape
    n = w_gate.shape[1]
    tm = _tile(m, tm)
    row = lambda i: (i, 0)
    return pl.pallas_call(
        functools.partial(_norm_gate_kernel, row_parts=row_parts, col_parts=col_parts),
        grid=(m // tm,),
        in_specs=[pl.BlockSpec((tm, d), row), pl.BlockSpec((1, d), lambda i: (0, 0)),
                  pl.BlockSpec((d, n), lambda i: (0, 0), pipeline_mode=pl.Buffered(1))],
        out_specs=[pl.BlockSpec((tm, d), row), pl.BlockSpec((tm, n), row)],
        out_shape=[jax.ShapeDtypeStruct((m, d), BF16), jax.ShapeDtypeStruct((m, n), BF16)],
        compiler_params=_params("parallel"),
        name="norm_gate",
    )(x, g.reshape(1, d), w_gate)


def _rope_tile(x, tab):
    t = x * tab
    return t + pltpu.roll(t, ROPE_DIM, 1)


def _qmla_kernel(a_ref, w_ref, tab_ref, g_ref, o_ref):
    a = a_ref[...]
    tab = tab_ref[...]
    low = (lax.broadcasted_iota(jnp.int32, (1, HEAD_DIM), 1) < ROPE_DIM).astype(F32)
    g_nope, g_rope = g_ref[:, :HEAD_DIM] * MLA_SCALE, g_ref[:, HEAD_DIM:] * MLA_SCALE
    for h in range(HEADS):
        c0 = h * QK_PAD
        acc = _dot(a, w_ref[:, c0:c0 + QK_PAD])
        nope = acc[:, :HEAD_DIM]
        rope = _rope_tile(acc[:, HEAD_DIM:], tab) * low
        ss = jnp.sum(nope * nope + rope * rope, axis=-1, keepdims=True)
        rs = lax.rsqrt(ss * (1.0 / QK_DIM) + NORM_EPS)
        o_ref[:, c0:c0 + HEAD_DIM] = (nope * rs * g_nope).astype(o_ref.dtype)
        o_ref[:, c0 + HEAD_DIM:c0 + QK_PAD] = (rope * rs * g_rope).astype(o_ref.dtype)


def _ckv_kernel(a_ref, w_ref, tab_ref, g_ref, ckv_ref, kr_ref, ckvb_ref, *, layer):
    a = a_ref[...]
    latent = ckvb_ref.shape[1]
    _layer_slab(kr_ref, layer)[...] = _rope_tile(_dot(a, w_ref[:, latent:]), tab_ref[...])[:, :ROPE_DIM]
    cn = _rms(_dot(a, w_ref[:, :latent]), g_ref[...])
    _layer_slab(ckv_ref, layer)[...] = cn
    ckvb_ref[...] = cn.astype(ckvb_ref.dtype)


def _project(x, tab, w, layer, new_cache, tm=1024):
    m, d = x.shape
    h, gate = _norm_gate(x, w["norm1_g"], w["gate"])
    tm = _tile(m, tm)
    width = HEADS * HEAD_DIM
    depth = w["depth"]
    latent = w["kv_norm_g"].shape[1]
    row = lambda i, j: (i, 0)
    full = lambda shape, dtype: (shape, dtype, (tm, shape[1]), row)
    carry = lambda idx, out: [] if new_cache is None else [(new_cache[idx], out)]

    def slab(rows, width, block_rows):
        if new_cache is None:
            return ((depth, rows, width), F32, (depth, block_rows, width), lambda i, j: (0, i, 0))
        return ((depth, rows, width), F32, (None, block_rows, width), lambda i, j: (layer, i, 0))

    tms = _tile(m, 512)
    cache_out = slab(m * HEADS, HEAD_DIM, tms * HEADS)
    half = ((m, width), BF16, (tms, width), row)
    q, k32, kb, v32, vb = _matmul_call(
        functools.partial(_sb_qkv_kernel, layer=layer), h, w["sb_qkv"], [], [half, cache_out, half, cache_out, half],
        tms, 3 * width, "proj_sb_qkv", carry(0, 1) + carry(1, 3), resident=True)
    tmq = _tile(m, 512)
    tab_spec = lambda t: (tab, (t, HEAD_DIM), row)
    gq = (w["q_norm_g"], (1, QK_PAD), lambda i, j: (0, 0))
    (qm,) = _matmul_call(_qmla_kernel, h, w["mla_q"], [tab_spec(tmq), gq],
                         [((m, HEADS * QK_PAD), BF16, (tmq, HEADS * QK_PAD), row)], tmq, HEADS * QK_PAD, "proj_mla_q")
    gkv = (w["kv_norm_g"], (1, latent), lambda i, j: (0, 0))
    ckv32, kr, ckvb = _matmul_call(
        functools.partial(_ckv_kernel, layer=layer), h, w["ckv_kr"], [tab_spec(tm), gkv],
        [slab(m, latent, tm), slab(m, ROPE_DIM, tm), full((m, latent), BF16)],
        tm, latent + HEAD_DIM, "proj_ckv", carry(2, 0) + carry(3, 1))
    return dict(q=q, kb=kb, vb=vb, qm=qm, ckvb=ckvb, gate=gate), (k32, v32, ckv32, kr)


def _kvproj_kernel(c_ref, kr_ref, wuk_ref, wuv_ref, g_ref, k_ref, v_ref):
    c = c_ref[...].astype(BF16)
    kn = _dot(c, wuk_ref[...])
    v_ref[...] = _dot(c, wuv_ref[...]).astype(v_ref.dtype)
    kr = kr_ref[...]
    ss_kr = jnp.sum(kr * kr, axis=-1, keepdims=True)
    kr_pad = jnp.concatenate([kr, jnp.zeros_like(kr)], axis=1)
    g_nope, g_rope = g_ref[:, :HEAD_DIM], g_ref[:, HEAD_DIM:]
    for h in range(HEADS):
        nope = kn[:, h * HEAD_DIM:(h + 1) * HEAD_DIM]
        ss = jnp.sum(nope * nope, axis=-1, keepdims=True) + ss_kr
        rs = lax.rsqrt(ss * (1.0 / QK_DIM) + NORM_EPS)
        c0 = h * QK_PAD
        k_ref[:, c0:c0 + HEAD_DIM] = (nope * rs * g_nope).astype(k_ref.dtype)
        k_ref[:, c0 + HEAD_DIM:c0 + QK_PAD] = (kr_pad * rs * g_rope).astype(k_ref.dtype)


def _kv_project(c, c_layer, kr, kr_layer, w, tm=512):
    _, rows, latent = c.shape
    tm = _tile(rows, tm)
    width = HEADS * HEAD_DIM
    const = lambda i: (0, 0)
    return pl.pallas_call(
        _kvproj_kernel,
        grid=(rows // tm,),
        in_specs=[pl.BlockSpec((None, tm, latent), lambda i: (c_layer, i, 0)),
                  pl.BlockSpec((None, tm, ROPE_DIM), lambda i: (kr_layer, i, 0)),
                  pl.BlockSpec((latent, width), const), pl.BlockSpec((latent, width), const),
                  pl.BlockSpec((1, QK_PAD), const)],
        out_specs=[pl.BlockSpec((tm, HEADS * QK_PAD), lambda i: (i, 0)), pl.BlockSpec((tm, width), lambda i: (i, 0))],
        out_shape=[jax.ShapeDtypeStruct((rows, HEADS * QK_PAD), BF16), jax.ShapeDtypeStruct((rows, width), BF16)],
        compiler_params=_params("parallel"),
        name="mla_kv_project",
    )(c, kr, w["uk"], w["uv"], w["k_norm_g"])


def _later_sum_matrix(n, doubled):
    rows = 2 * n if doubled else n
    r = lax.broadcasted_iota(jnp.int32, (rows, n), 0)
    c = lax.broadcasted_iota(jnp.int32, (rows, n), 1)
    if doubled:
        r = jnp.where(r >= n, r - n, r)
    return jnp.where(r > c, 1.0, 0.0).astype(BF16)


def _sb_block(q, k, v, tri, rest, acc, mask):
    z = _nt_dot(q, k)
    sp = jnp.maximum(z, 0.0) + jnp.log2(1.0 + jnp.exp2(-jnp.abs(z)))
    if mask is not None:
        sp = jnp.where(mask, sp, 0.0)
    hi = sp.astype(BF16)
    lo = (sp - hi.astype(F32)).astype(BF16)
    if tri.shape[0] == 2 * tri.shape[1]:
        later = _dot(jnp.concatenate([hi, lo], axis=1), tri)
    else:
        later = _dot(hi, tri) + _dot(lo, tri)
    a = jnp.exp2(((z - sp) - later) - rest)
    if mask is not None:
        a = jnp.where(mask, a, 0.0)
    acc = acc + _dot(a.astype(BF16), v)
    rest = rest + jnp.sum(sp, axis=-1, keepdims=True)
    return rest, acc


def _unfinished(rest):
    return (jnp.min(rest) < SB_DONE).astype(jnp.int32)


def _causal_mask(n):
    t_idx = lax.broadcasted_iota(jnp.int32, (n, n), 0)
    s_idx = lax.broadcasted_iota(jnp.int32, (n, n), 1)
    return s_idx < t_idx


def _sb_prompt_kernel(q_ref, k_ref, v_ref, o_ref, *, blk):
    s_len = q_ref.shape[1]
    tri = _later_sum_matrix(blk, blk % HEAD_DIM == 0)
    causal = _causal_mask(blk)

    def kv(j0):
        return k_ref[0, pl.ds(j0, blk), :], v_ref[0, pl.ds(j0, blk), :]

    def nearest(q0, has_prev):
        q = q_ref[0, pl.ds(q0, blk), :]
        rest, acc = _sb_block(q, *kv(q0), tri, jnp.zeros((blk, 1), F32), jnp.zeros((blk, HEAD_DIM), F32), causal)
        if has_prev:
            rest, acc = _sb_block(q, *kv(q0 - blk), tri, rest, acc, None)
        return q, rest, acc

    def further(q, j_first, rest, acc):
        def cond(c):
            return jnp.logical_and(c[0] >= 0, c[1] > 0)

        def body(c):
            j, _, rest, acc = c
            rest, acc = _sb_block(q, *kv(pl.multiple_of(j * blk, blk)), tri, rest, acc, None)
            return j - 1, _unfinished(rest), rest, acc

        return lax.while_loop(cond, body, (j_first, _unfinished(rest), rest, acc))[3]

    starts = [i * blk for i in range(s_len // blk)]
    states = [nearest(q0, q0 > 0) for q0 in starts]
    unfinished = functools.reduce(jnp.maximum, [_unfinished(rest) for _, rest, _ in states])
    accs = lax.cond(
        unfinished > 0,
        lambda: tuple(further(q, i - 2, rest, acc) for i, (q, rest, acc) in enumerate(states)),
        lambda: tuple(acc for _, _, acc in states))
    for q0, acc in zip(starts, accs):
        o_ref[0, pl.ds(q0, blk), :] = acc.astype(o_ref.dtype)


def _sb_prompt(q, k, v, blk=256):
    b, s, _ = q.shape
    blk = _tile(s, blk)
    spec = pl.BlockSpec((1, s, HEAD_DIM), lambda bi, h: (bi, 0, h))
    return pl.pallas_call(
        functools.partial(_sb_prompt_kernel, blk=blk),
        grid=(b, HEADS),
        in_specs=[spec, spec, spec],
        out_specs=spec,
        out_shape=jax.ShapeDtypeStruct(q.shape, BF16),
        compiler_params=_params("parallel", "parallel"),
        name="sb_prompt",
    )(q, k, v)


def _sb_sample_kernel(q_ref, kn_ref, vn_ref, kc_ref, vc_ref, o_ref, kbuf, vbuf, sem, rest_ref, acc_ref,
                      *, layer, chunk, blk):
    b = pl.program_id(0)
    n = q_ref.shape[1]
    nc = kc_ref.shape[2] // (chunk * HEADS)
    cols = lambda h: slice(h * HEAD_DIM, (h + 1) * HEAD_DIM)

    def copies(c):
        rows = pl.ds(pl.multiple_of((nc - 1 - c) * (chunk * HEADS), chunk * HEADS), chunk * HEADS)
        return (pltpu.make_async_copy(kc_ref.at[layer, b, rows, :], kbuf, sem.at[0]),
                pltpu.make_async_copy(vc_ref.at[layer, b, rows, :], vbuf, sem.at[1]))

    for cp in copies(0):
        cp.start()

    tri_new = _later_sum_matrix(n, n % HEAD_DIM == 0)
    causal = _causal_mask(n)
    for h in range(HEADS):
        rest, acc = _sb_block(q_ref[0, :, cols(h)], kn_ref[0, :, cols(h)], vn_ref[0, :, cols(h)], tri_new,
                              jnp.zeros((n, 1), F32), jnp.zeros((n, HEAD_DIM), F32), causal)
        rest_ref[h] = rest
        acc_ref[h] = acc

    tri = _later_sum_matrix(blk, blk % HEAD_DIM == 0)

    def chunk_body(carry):
        c, _ = carry
        for cp in copies(c):
            cp.wait()
        live = jnp.int32(0)
        for h in range(HEADS):
            q = q_ref[0, :, cols(h)]
            rest, acc = rest_ref[h], acc_ref[h]
            for jb in reversed(range(chunk // blk)):
                rows = pl.ds(jb * blk * HEADS + h, blk, stride=HEADS)
                rest, acc = _sb_block(q, kbuf[rows, :].astype(BF16), vbuf[rows, :].astype(BF16), tri, rest, acc, None)
            rest_ref[h] = rest
            acc_ref[h] = acc
            live = jnp.maximum(live, _unfinished(rest))
        more = jnp.logical_and(c + 1 < nc, live > 0)

        @pl.when(more)
        def _():
            for cp in copies(c + 1):
                cp.start()

        return c + 1, more.astype(jnp.int32)

    lax.while_loop(lambda carry: carry[1] > 0, chunk_body, (jnp.int32(0), jnp.int32(1)))

    for h in range(HEADS):
        o_ref[0, :, cols(h)] = acc_ref[h].astype(o_ref.dtype)


def _sb_sample(q, k_new, v_new, cache_k, cache_v, layer, chunk=256, blk=256):
    b, n, width = q.shape
    past = cache_k.shape[2] // HEADS
    assert past > 0
    chunk = _tile(past, chunk)
    blk = _tile(chunk, blk)
    new = pl.BlockSpec((1, n, width), lambda bi: (bi, 0, 0))
    hbm = pl.BlockSpec(memory_space=pl.ANY)
    return pl.pallas_call(
        functools.partial(_sb_sample_kernel, layer=layer, chunk=chunk, blk=blk),
        grid=(b,),
        in_specs=[new, new, new, hbm, hbm],
        out_specs=new,
        out_shape=jax.ShapeDtypeStruct(q.shape, BF16),
        scratch_shapes=[pltpu.VMEM((chunk * HEADS, HEAD_DIM), F32), pltpu.VMEM((chunk * HEADS, HEAD_DIM), F32),
                        pltpu.SemaphoreType.DMA((2,)),
                        pltpu.VMEM((HEADS, n, 1), F32), pltpu.VMEM((HEADS, n, HEAD_DIM), F32)],
        compiler_params=_params("parallel"),
        name="sb_sample",
    )(q, k_new, v_new, cache_k, cache_v)


def _mla_advance(state, st, v_prev, mask):
    m, l, acc, p_prev = state
    pv = _tn_dot(v_prev, p_prev)
    if mask is not None:
        st = jnp.where(mask, st, NEG_INF)
    m_new = jnp.maximum(m, jnp.max(st, axis=0, keepdims=True))
    alpha = jnp.exp2(m - m_new)
    p = jnp.exp2(st - m_new)
    l = alpha * l + jnp.sum(p, axis=0, keepdims=True)
    return m_new, l, alpha * (acc + pv), p.astype(BF16)


def _mla_prompt_kernel(q_ref, k_ref, v_ref, o_ref, *, blk):
    s_len = q_ref.shape[1]
    k_idx = lax.broadcasted_iota(jnp.int32, (blk, blk), 0)
    q_idx = lax.broadcasted_iota(jnp.int32, (blk, blk), 1)
    chunk_mask = (k_idx // CHUNK) <= (q_idx // CHUNK)
    init = (jnp.full((1, blk), NEG_INF, F32), jnp.zeros((1, blk), F32), jnp.zeros((HEAD_DIM, blk), F32),
            jnp.zeros((blk, blk), BF16))

    def rows(j):
        return pl.ds(j * blk, blk)

    for i in range(s_len // blk):
        q = q_ref[0, rows(i), :]
        state = init
        for j in range(i + 1):
            st = _nt_dot(k_ref[0, rows(j), :], q)
            state = _mla_advance(state, st, v_ref[0, rows(max(j - 1, 0)), :], chunk_mask if j == i else None)
        _, l, acc, p_last = state
        acc = acc + _tn_dot(v_ref[0, rows(i), :], p_last)
        o_ref[0, rows(i), :] = (acc / l).T.astype(o_ref.dtype)


def _mla_prompt(q, k, v, blk=512):
    b, s, _ = q.shape
    blk = _tile(s, blk)
    assert blk % CHUNK == 0
    qk = pl.BlockSpec((1, s, QK_PAD), lambda bi, h: (bi, 0, h))
    vo = pl.BlockSpec((1, s, HEAD_DIM), lambda bi, h: (bi, 0, h))
    return pl.pallas_call(
        functools.partial(_mla_prompt_kernel, blk=blk),
        grid=(b, HEADS),
        in_specs=[qk, qk, vo],
        out_specs=vo,
        out_shape=jax.ShapeDtypeStruct(v.shape, BF16),
        compiler_params=_params("parallel", "parallel"),
        name="mla_prompt",
    )(q, k, v)


def _mla_sample_kernel(q_ref, cp_ref, rp_ref, cn_ref, rn_ref, wuk_ref, wukt_ref, wuv_ref, g_ref, o_ref,
                       *, past, chunk):
    n = q_ref.shape[1]
    g_nope, g_rope = g_ref[:, :HEAD_DIM], g_ref[:, HEAD_DIM:]
    rows = lambda h: slice(h * n, (h + 1) * n)

    qa, qr = [], []
    for h in range(HEADS):
        c0 = h * QK_PAD
        q_nope = (q_ref[0, :, c0:c0 + HEAD_DIM].astype(F32) * g_nope).astype(BF16)
        qa.append(_nt_dot(q_nope, wuk_ref[:, h * HEAD_DIM:(h + 1) * HEAD_DIM]).astype(BF16))
        q_rope = q_ref[0, :, c0 + HEAD_DIM:c0 + QK_PAD].astype(F32) * g_rope
        qr.append(q_rope[:, :ROPE_DIM].astype(BF16))
    lhs = jnp.concatenate(qa + [wukt_ref[...]], axis=0)
    qr = jnp.concatenate(qr, axis=0)

    def scores(c, kr):
        both = _nt_dot(lhs, c)
        s_rope = _nt_dot(qr, kr.astype(BF16))
        kr_t = kr.T
        ss_kr = jnp.sum(kr_t * kr_t, axis=0, keepdims=True)
        out = []
        for h in range(HEADS):
            kn_t = both[HEADS * n + h * HEAD_DIM:HEADS * n + (h + 1) * HEAD_DIM]
            ss = jnp.sum(kn_t * kn_t, axis=0, keepdims=True) + ss_kr
            rs = lax.rsqrt(ss * (1.0 / QK_DIM) + NORM_EPS)
            out.append((both[rows(h)] + s_rope[rows(h)]) * rs)
        return out

    c_new = cn_ref[...]
    k_pos = past + lax.broadcasted_iota(jnp.int32, (n, n), 1)
    q_pos = past + lax.broadcasted_iota(jnp.int32, (n, n), 0)
    visible = (k_pos // CHUNK) <= (q_pos // CHUNK)
    s = jnp.concatenate([jnp.where(visible, s_h, NEG_INF) for s_h in scores(c_new, rn_ref[...])], axis=0)
    m = jnp.max(s, axis=-1, keepdims=True)
    p = jnp.exp2(s - m)
    l = jnp.sum(p, axis=-1, keepdims=True)
    mix = _dot(p.astype(BF16), c_new)
    for j in range(past // chunk):
        ks = slice(j * chunk, (j + 1) * chunk)
        c = cp_ref[ks, :].astype(BF16)
        s = jnp.concatenate(scores(c, rp_ref[ks, :]), axis=0)
        m_new = jnp.maximum(m, jnp.max(s, axis=-1, keepdims=True))
        alpha = jnp.exp2(m - m_new)
        p = jnp.exp2(s - m_new)
        l = alpha * l + jnp.sum(p, axis=-1, keepdims=True)
        mix = alpha * mix + _dot(p.astype(BF16), c)
        m = m_new
    mix = (mix / l).astype(BF16)
    for h in range(HEADS):
        cols = slice(h * HEAD_DIM, (h + 1) * HEAD_DIM)
        o_ref[0, :, cols] = _dot(mix[rows(h)], wuv_ref[:, cols]).astype(o_ref.dtype)


def _mla_sample(q, cache_c, cache_r, layer, c_new, r_new, r_layer, w, chunk=1024):
    b, n, _ = q.shape
    latent = cache_c.shape[2]
    past = cache_c.shape[1] // b
    chunk = _tile(past, chunk)
    width = HEADS * HEAD_DIM
    const = lambda bi: (0, 0)
    return pl.pallas_call(
        functools.partial(_mla_sample_kernel, past=past, chunk=chunk),
        grid=(b,),
        in_specs=[pl.BlockSpec((1, n, HEADS * QK_PAD), lambda bi: (bi, 0, 0)),
                  pl.BlockSpec((None, past, latent), lambda bi: (layer, bi, 0)),
                  pl.BlockSpec((None, past, ROPE_DIM), lambda bi: (layer, bi, 0)),
                  pl.BlockSpec((n, latent), lambda bi: (bi, 0)),
                  pl.BlockSpec((None, n, ROPE_DIM), lambda bi: (r_layer, bi, 0)),
                  pl.BlockSpec((latent, width), const), pl.BlockSpec((width, latent), const),
                  pl.BlockSpec((latent, width), const), pl.BlockSpec((1, QK_PAD), const)],
        out_specs=pl.BlockSpec((1, n, width), lambda bi: (bi, 0, 0)),
        out_shape=jax.ShapeDtypeStruct((b, n, width), BF16),
        compiler_params=_params("parallel"),
        name="mla_sample",
    )(q, cache_c, cache_r, c_new, r_new, w["uk"], w["uk_t"], w["uv"], w["k_norm_g"])


def _merge_kernel(a1_ref, a2_ref, g_ref, x_ref, w1_ref, w2_ref, wo_ref, o_ref):
    d = o_ref.shape[1]
    y1 = _dot(a1_ref[...], w1_ref[...])
    y2 = _dot(a2_ref[...], w2_ref[...])
    mix = g_ref[:, :d].astype(F32) * y1 + g_ref[:, d:].astype(F32) * y2
    o_ref[...] = x_ref[...] + _dot(mix.astype(BF16), wo_ref[...])


def _merge_residual(sb_out, mla_out, gate, x, w, tm=512):
    m, k = sb_out.shape
    d = x.shape[1]
    tm = _tile(m, tm)
    row = lambda width: pl.BlockSpec((tm, width), lambda i: (i, 0))
    layer = w["layer"]
    resident = lambda shape: pl.BlockSpec((None,) + shape, lambda i: (layer, 0, 0), pipeline_mode=pl.Buffered(1))
    return pl.pallas_call(
        _merge_kernel,
        grid=(m // tm,),
        in_specs=[row(k), row(k), row(2 * d), row(d), resident((k, d)), resident((k, d)), resident((d, d))],
        out_specs=row(d),
        out_shape=jax.ShapeDtypeStruct((m, d), F32),
        compiler_params=_params("parallel"),
        name="merge_residual",
    )(sb_out, mla_out, gate, x, w["sb_proj"], w["mla_proj"], w["o"])


def _ffn_kernel(x_ref, g_ref, wup_ref, wdn_ref, o_ref, h_ref):
    @pl.when(pl.program_id(1) == 0)
    def _():
        x = x_ref[...]
        h_ref[...] = _rms(x, g_ref[...]).astype(h_ref.dtype)
        o_ref[...] = x

    u = _dot(h_ref[...], wup_ref[...])
    u = jnp.square(jnp.maximum(u, 0.0)).astype(BF16)
    o_ref[...] += _dot(u, wdn_ref[...])


def _ffn(x, w, tm=1024, tf=1024):
    m, d = x.shape
    f = w["up"].shape[2]
    layer = w["layer"]
    tm, tf = _tile(m, tm), _tile(f, tf)
    return pl.pallas_call(
        _ffn_kernel,
        grid=(m // tm, f // tf),
        in_specs=[pl.BlockSpec((tm, d), lambda i, j: (i, 0)), pl.BlockSpec((1, d), lambda i, j: (0, 0)),
                  pl.BlockSpec((None, d, tf), lambda i, j: (layer, 0, j)),
                  pl.BlockSpec((None, tf, d), lambda i, j: (layer, j, 0))],
        out_specs=pl.BlockSpec((tm, d), lambda i, j: (i, 0)),
        out_shape=jax.ShapeDtypeStruct((m, d), F32),
        scratch_shapes=[pltpu.VMEM((tm, d), BF16)],
        compiler_params=_params("parallel", "arbitrary", vmem_limit_bytes=FFN_VMEM_LIMIT_BYTES),
        name="ffn",
    )(x, w["norm2_g"], w["up"], w["down"])


def _rope_table(pos):
    inv_freq = ROPE_THETA ** (-jnp.arange(HALF_ROPE, dtype=F32) / HALF_ROPE)
    ang = pos.astype(F32)[:, None] * inv_freq[None, :]
    cos, sin = jnp.cos(ang), jnp.sin(ang)
    return jnp.concatenate([cos, cos, -sin, sin], axis=1)


def _rope_cols(w):
    x1, x2 = w[..., :HALF_ROPE], w[..., HALF_ROPE:]
    return jnp.concatenate([x1, x2, x2, x1], axis=-1)


def _pad_gain(g):
    return jnp.concatenate([g, jnp.zeros((QK_PAD - QK_DIM,), g.dtype)]).reshape(1, QK_PAD)


def _layer_weights(l, stacks, norm1_g, w_in, q_norm_g, k_norm_g, kv_norm_g, w_uk, w_uv, norm2_g):
    depth, d = w_in.shape[:2]
    width = HEADS * HEAD_DIM
    latent = w_uk.shape[1]
    wi = w_in[l]
    o_q, o_ckv = 3 * width, 3 * width + HEADS * QK_DIM
    o_kr, o_gate = o_ckv + latent, o_ckv + latent + ROPE_DIM
    wq = wi[:, o_q:o_ckv].reshape(d, HEADS, QK_DIM)
    wq = jnp.concatenate([wq[..., :HEAD_DIM], _rope_cols(wq[..., HEAD_DIM:])], axis=-1).reshape(d, HEADS * QK_PAD)
    bf = lambda a: a.astype(BF16)
    return {
        "depth": depth, "layer": l, **stacks,
        "norm1_g": norm1_g[l], "norm2_g": norm2_g[l].reshape(1, d),
        "sb_qkv": bf(wi[:, :o_q]),
        "mla_q": bf(wq),
        "ckv_kr": bf(jnp.concatenate([wi[:, o_ckv:o_kr], _rope_cols(wi[:, o_kr:o_gate])], axis=1)),
        "gate": bf(wi[:, o_gate:]),
        "q_norm_g": _pad_gain(q_norm_g[l]), "k_norm_g": _pad_gain(k_norm_g[l]),
        "kv_norm_g": kv_norm_g[l].reshape(1, latent),
        "uk": bf(w_uk[l]), "uk_t": bf(w_uk[l].T), "uv": bf(w_uv[l]),
    }


def _merge_and_ffn(x, sb_out, mla_out, gate, w):
    return _ffn(_merge_residual(sb_out, mla_out, gate, x, w), w)


def kernel(x_prompt, x_sample, cache_sb_k, cache_sb_v, cache_mla_ckv, cache_mla_krope, norm1_g, w_in, q_norm_g, k_norm_g, kv_norm_g, w_uk, w_uv, w_sb_proj, w_mla_proj, w_o, norm2_g, w_up, w_down):
    b, s, d = x_prompt.shape
    bs, n, _ = x_sample.shape
    depth, _, past = cache_sb_k.shape[:3]
    width = HEADS * HEAD_DIM
    latent = cache_mla_ckv.shape[-1]

    tab_p = jnp.tile(_rope_table(jnp.arange(s, dtype=jnp.int32)), (b, 1))
    tab_s = jnp.tile(_rope_table(past + jnp.arange(n, dtype=jnp.int32)), (bs, 1))
    cache_k = cache_sb_k.reshape(depth, bs, past * HEADS, HEAD_DIM)
    cache_v = cache_sb_v.reshape(depth, bs, past * HEADS, HEAD_DIM)
    cache_c = cache_mla_ckv.reshape(depth, bs * past, latent)
    cache_r = cache_mla_krope.reshape(depth, bs * past, ROPE_DIM)

    xp = x_prompt.reshape(b * s, d)
    xs = x_sample.reshape(bs * n, d)
    new_p = new_s = None
    stacks = {"sb_proj": w_sb_proj.astype(BF16), "mla_proj": w_mla_proj.astype(BF16), "o": w_o.astype(BF16),
              "up": w_up.astype(BF16), "down": w_down.astype(BF16)}
    seq = lambda a: a.reshape(b, s, a.shape[1])
    dec = lambda a: a.reshape(bs, n, a.shape[1])
    for l in range(depth):
        w = _layer_weights(l, stacks, norm1_g, w_in, q_norm_g, k_norm_g, kv_norm_g, w_uk, w_uv, norm2_g)
        q, new_s = _project(xs, tab_s, w, l, new_s)
        sb_out = _sb_sample(dec(q["q"]), dec(q["kb"]), dec(q["vb"]), cache_k, cache_v, l)
        mla_out = _mla_sample(dec(q["qm"]), cache_c, cache_r, l, q["ckvb"], new_s[3], l, w)
        xs = _merge_and_ffn(xs, sb_out.reshape(bs * n, width), mla_out.reshape(bs * n, width), q["gate"], w)
        p, new_p = _project(xp, tab_p, w, l, new_p)
        sb_out = _sb_prompt(seq(p["q"]), seq(p["kb"]), seq(p["vb"]))
        k_mla, v_mla = _kv_project(p["ckvb"][None], 0, new_p[3], l, w)
        mla_out = _mla_prompt(seq(p["qm"]), seq(k_mla), seq(v_mla))
        xp = _merge_and_ffn(xp, sb_out.reshape(b * s, width), mla_out.reshape(b * s, width), p["gate"], w)

    def caches(new, bb, t):
        k32, v32, ckv, kr = new
        return (k32.reshape(depth, bb, t, HEADS, HEAD_DIM), v32.reshape(depth, bb, t, HEADS, HEAD_DIM),
                ckv.reshape(depth, bb, t, latent), kr.reshape(depth, bb, t, ROPE_DIM))

    return (xp.reshape(b, s, d), xs.reshape(bs, n, d), *caches(new_p, b, s), *caches(new_s, bs, n))
```

```python
import functools

import jax
import jax.numpy as jnp
from jax import lax
from jax.experimental import pallas as pl
from jax.experimental.pallas import tpu as pltpu

F32 = jnp.float32
BF16 = jnp.bfloat16

CHUNK = 64
HEADS = 8
HEAD_DIM = 128
ROPE_DIM = 64
HALF_ROPE = ROPE_DIM // 2
QK_DIM = HEAD_DIM + ROPE_DIM
QK_PAD = 2 * HEAD_DIM
ROPE_THETA = 10000.0
NORM_EPS = 1e-6
NEG_INF = -1e30
LOG2E = 1.4426950408889634
SB_SCALE = LOG2E * HEAD_DIM ** -0.5
MLA_SCALE = LOG2E * QK_DIM ** -0.5
SB_DONE = 110.0 * LOG2E

MIB = 1024 * 1024
VMEM_LIMIT_BYTES = 56 * MIB
FFN_VMEM_LIMIT_BYTES = 62 * MIB


def _params(*sem, vmem_limit_bytes=VMEM_LIMIT_BYTES):
    return pltpu.CompilerParams(dimension_semantics=sem, vmem_limit_bytes=vmem_limit_bytes)


def _tile(m, pref):
    t = min(m, pref)
    assert m % t == 0, (m, pref)
    return t


def _nt_dot(a, b):
    return lax.dot_general(a, b, (((1,), (1,)), ((), ())), preferred_element_type=F32)


def _tn_dot(a, b):
    return lax.dot_general(a, b, (((0,), (0,)), ((), ())), preferred_element_type=F32)


def _dot(a, b):
    return jnp.dot(a, b, preferred_element_type=F32)


def _rms(x, g):
    ms = jnp.mean(x * x, axis=-1, keepdims=True)
    return x * lax.rsqrt(ms + NORM_EPS) * g


def _matmul_call(body, a, w, extras, outs, tm, tn, name, carried=(), resident=False):
    m, k = a.shape
    n = w.shape[1]
    tm, tn = _tile(m, tm), _tile(n, tn)
    n_in = 2 + len(extras)
    w_mode = {"pipeline_mode": pl.Buffered(1)} if resident else {}
    in_specs = [pl.BlockSpec((tm, k), lambda i, j: (i, 0)), pl.BlockSpec((k, tn), lambda i, j: (0, j), **w_mode)]
    in_specs += [pl.BlockSpec(bs, im) for _, bs, im in extras]
    in_specs += [pl.BlockSpec(memory_space=pl.ANY)] * len(carried)

    def kernel_fn(*refs):
        body(*refs[:n_in], *refs[n_in + len(carried):])

    return pl.pallas_call(
        kernel_fn,
        grid=(m // tm, n // tn),
        in_specs=in_specs,
        out_specs=[pl.BlockSpec(bs, im) for _, _, bs, im in outs],
        out_shape=[jax.ShapeDtypeStruct(s, d) for s, d, _, _ in outs],
        input_output_aliases={n_in + t: oi for t, (_, oi) in enumerate(carried)},
        compiler_params=_params("parallel", "arbitrary"),
        name=name,
    )(a, w, *[e[0] for e in extras], *[c[0] for c in carried])


def _layer_slab(ref, layer):
    if len(ref.shape) == 2:
        return ref
    for other in range(ref.shape[0]):
        if other != layer:
            ref[other] = jnp.zeros(ref.shape[1:], ref.dtype)
    return ref.at[layer]


def _sb_qkv_kernel(a_ref, w_ref, q_ref, kc_ref, kb_ref, vc_ref, vb_ref, *, layer):
    a = a_ref[...]
    tm = a.shape[0]
    width = q_ref.shape[1]
    q_ref[...] = (_dot(a, w_ref[:, :width]) * SB_SCALE).astype(q_ref.dtype)
    for part, (cache_ref, o_ref) in enumerate(((kc_ref, kb_ref), (vc_ref, vb_ref)), start=1):
        acc = _dot(a, w_ref[:, part * width:(part + 1) * width])
        o_ref[...] = acc.astype(o_ref.dtype)
        slab = _layer_slab(cache_ref, layer)
        for h in range(HEADS):
            slab[pl.ds(h, tm, stride=HEADS), :] = acc[:, h * HEAD_DIM:(h + 1) * HEAD_DIM]


def _norm_gate_kernel(x_ref, g_ref, w_ref, h_ref, o_ref, *, row_parts, col_parts):
    rp = x_ref.shape[0] // row_parts
    cp = w_ref.shape[1] // col_parts
    for r in range(row_parts):
        rows = slice(r * rp, (r + 1) * rp)
        h = _rms(x_ref[rows, :], g_ref[...]).astype(h_ref.dtype)
        h_ref[rows, :] = h
        for c in range(col_parts):
            cols = slice(c * cp, (c + 1) * cp)
            o_ref[rows, cols] = jax.nn.sigmoid(_dot(h, w_ref[:, cols])).astype(o_ref.dtype)


def _norm_gate(x, g, w_gate, tm=512, row_parts=2, col_parts=4):
    m, d = x.shape
    n = w_gate.shape[1]
    tm = _tile(m, tm)
    row = lambda i: (i, 0)
    return pl.pallas_call(
        functools.partial(_norm_gate_kernel, row_parts=row_parts, col_parts=col_parts),
        grid=(m // tm,),
        in_specs=[pl.BlockSpec((tm, d), row), pl.BlockSpec((1, d), lambda i: (0, 0)),
                  pl.BlockSpec((d, n), lambda i: (0, 0), pipeline_mode=pl.Buffered(1))],
        out_specs=[pl.BlockSpec((tm, d), row), pl.BlockSpec((tm, n), row)],
        out_shape=[jax.ShapeDtypeStruct((m, d), BF16), jax.ShapeDtypeStruct((m, n), BF16)],
        compiler_params=_params("parallel"),
        name="norm_gate",
    )(x, g.reshape(1, d), w_gate)


def _rope_tile(x, tab):
    t = x * tab
    return t + pltpu.roll(t, ROPE_DIM, 1)


def _qmla_kernel(a_ref, w_ref, tab_ref, g_ref, o_ref):
    a = a_ref[...]
    tab = tab_ref[...]
    low = (lax.broadcasted_iota(jnp.int32, (1, HEAD_DIM), 1) < ROPE_DIM).astype(F32)
    g_nope, g_rope = g_ref[:, :HEAD_DIM] * MLA_SCALE, g_ref[:, HEAD_DIM:] * MLA_SCALE
    for h in range(HEADS):
        c0 = h * QK_PAD
        acc = _dot(a, w_ref[:, c0:c0 + QK_PAD])
        nope = acc[:, :HEAD_DIM]
        rope = _rope_tile(acc[:, HEAD_DIM:], tab) * low
        ss = jnp.sum(nope * nope + rope * rope, axis=-1, keepdims=True)
        rs = lax.rsqrt(ss * (1.0 / QK_DIM) + NORM_EPS)
        o_ref[:, c0:c0 + HEAD_DIM] = (nope * rs * g_nope).astype(o_ref.dtype)
        o_ref[:, c0 + HEAD_DIM:c0 + QK_PAD] = (rope * rs * g_rope).astype(o_ref.dtype)


def _ckv_kernel(a_ref, w_ref, tab_ref, g_ref, ckv_ref, kr_ref, ckvb_ref, *, layer):
    a = a_ref[...]
    latent = ckvb_ref.shape[1]
    _layer_slab(kr_ref, layer)[...] = _rope_tile(_dot(a, w_ref[:, latent:]), tab_ref[...])[:, :ROPE_DIM]
    cn = _rms(_dot(a, w_ref[:, :latent]), g_ref[...])
    _layer_slab(ckv_ref, layer)[...] = cn
    ckvb_ref[...] = cn.astype(ckvb_ref.dtype)


def _project(x, tab, w, layer, new_cache, tm=1024):
    m, d = x.shape
    h, gate = _norm_gate(x, w["norm1_g"], w["gate"])
    tm = _tile(m, tm)
    width = HEADS * HEAD_DIM
    depth = w["depth"]
    latent = w["kv_norm_g"].shape[1]
    row = lambda i, j: (i, 0)
    full = lambda shape, dtype: (shape, dtype, (tm, shape[1]), row)
    carry = lambda idx, out: [] if new_cache is None else [(new_cache[idx], out)]

    def slab(rows, width, block_rows):
        if new_cache is None:
            return ((depth, rows, width), F32, (depth, block_rows, width), lambda i, j: (0, i, 0))
        return ((depth, rows, width), F32, (None, block_rows, width), lambda i, j: (layer, i, 0))

    tms = _tile(m, 512)
    cache_out = slab(m * HEADS, HEAD_DIM, tms * HEADS)
    half = ((m, width), BF16, (tms, width), row)
    q, k32, kb, v32, vb = _matmul_call(
        functools.partial(_sb_qkv_kernel, layer=layer), h, w["sb_qkv"], [], [half, cache_out, half, cache_out, half],
        tms, 3 * width, "proj_sb_qkv", carry(0, 1) + carry(1, 3), resident=True)
    tmq = _tile(m, 512)
    tab_spec = lambda t: (tab, (t, HEAD_DIM), row)
    gq = (w["q_norm_g"], (1, QK_PAD), lambda i, j: (0, 0))
    (qm,) = _matmul_call(_qmla_kernel, h, w["mla_q"], [tab_spec(tmq), gq],
                         [((m, HEADS * QK_PAD), BF16, (tmq, HEADS * QK_PAD), row)], tmq, HEADS * QK_PAD, "proj_mla_q")
    gkv = (w["kv_norm_g"], (1, latent), lambda i, j: (0, 0))
    ckv32, kr, ckvb = _matmul_call(
        functools.partial(_ckv_kernel, layer=layer), h, w["ckv_kr"], [tab_spec(tm), gkv],
        [slab(m, latent, tm), slab(m, ROPE_DIM, tm), full((m, latent), BF16)],
        tm, latent + HEAD_DIM, "proj_ckv", carry(2, 0) + carry(3, 1))
    return dict(q=q, kb=kb, vb=vb, qm=qm, ckvb=ckvb, gate=gate), (k32, v32, ckv32, kr)


def _kvproj_kernel(c_ref, kr_ref, wuk_ref, wuv_ref, g_ref, k_ref, v_ref):
    c = c_ref[...].astype(BF16)
    kn = _dot(c, wuk_ref[...])
    v_ref[...] = _dot(c, wuv_ref[...]).astype(v_ref.dtype)
    kr = kr_ref[...]
    ss_kr = jnp.sum(kr * kr, axis=-1, keepdims=True)
    kr_pad = jnp.concatenate([kr, jnp.zeros_like(kr)], axis=1)
    g_nope, g_rope = g_ref[:, :HEAD_DIM], g_ref[:, HEAD_DIM:]
    for h in range(HEADS):
        nope = kn[:, h * HEAD_DIM:(h + 1) * HEAD_DIM]
        ss = jnp.sum(nope * nope, axis=-1, keepdims=True) + ss_kr
        rs = lax.rsqrt(ss * (1.0 / QK_DIM) + NORM_EPS)
        c0 = h * QK_PAD
        k_ref[:, c0:c0 + HEAD_DIM] = (nope * rs * g_nope).astype(k_ref.dtype)
        k_ref[:, c0 + HEAD_DIM:c0 + QK_PAD] = (kr_pad * rs * g_rope).astype(k_ref.dtype)


def _kv_project(c, c_layer, kr, kr_layer, w, tm=1024):
    _, rows, latent = c.shape
    tm = _tile(rows, tm)
    width = HEADS * HEAD_DIM
    const = lambda i: (0, 0)
    return pl.pallas_call(
        _kvproj_kernel,
        grid=(rows // tm,),
        in_specs=[pl.BlockSpec((None, tm, latent), lambda i: (c_layer, i, 0)),
                  pl.BlockSpec((None, tm, ROPE_DIM), lambda i: (kr_layer, i, 0)),
                  pl.BlockSpec((latent, width), const), pl.BlockSpec((latent, width), const),
                  pl.BlockSpec((1, QK_PAD), const)],
        out_specs=[pl.BlockSpec((tm, HEADS * QK_PAD), lambda i: (i, 0)), pl.BlockSpec((tm, width), lambda i: (i, 0))],
        out_shape=[jax.ShapeDtypeStruct((rows, HEADS * QK_PAD), BF16), jax.ShapeDtypeStruct((rows, width), BF16)],
        compiler_params=_params("parallel"),
        name="mla_kv_project",
    )(c, kr, w["uk"], w["uv"], w["k_norm_g"])


def _later_sum_matrix(n, doubled):
    rows = 2 * n if doubled else n
    r = lax.broadcasted_iota(jnp.int32, (rows, n), 0)
    c = lax.broadcasted_iota(jnp.int32, (rows, n), 1)
    if doubled:
        r = jnp.where(r >= n, r - n, r)
    return jnp.where(r > c, 1.0, 0.0).astype(BF16)


def _sb_block(q, k, v, tri, rest, acc, mask):
    z = _nt_dot(q, k)
    sp = jnp.maximum(z, 0.0) + jnp.log2(1.0 + jnp.exp2(-jnp.abs(z)))
    if mask is not None:
        sp = jnp.where(mask, sp, 0.0)
    hi = sp.astype(BF16)
    lo = (sp - hi.astype(F32)).astype(BF16)
    if tri.shape[0] == 2 * tri.shape[1]:
        later = _dot(jnp.concatenate([hi, lo], axis=1), tri)
    else:
        later = _dot(hi, tri) + _dot(lo, tri)
    a = jnp.exp2(((z - sp) - later) - rest)
    if mask is not None:
        a = jnp.where(mask, a, 0.0)
    acc = acc + _dot(a.astype(BF16), v)
    rest = rest + jnp.sum(sp, axis=-1, keepdims=True)
    return rest, acc


def _unfinished(rest):
    return (jnp.min(rest) < SB_DONE).astype(jnp.int32)


def _causal_mask(n):
    t_idx = lax.broadcasted_iota(jnp.int32, (n, n), 0)
    s_idx = lax.broadcasted_iota(jnp.int32, (n, n), 1)
    return s_idx < t_idx


def _sb_prompt_kernel(q_ref, k_ref, v_ref, o_ref, *, blk):
    s_len = q_ref.shape[1]
    tri = _later_sum_matrix(blk, blk % HEAD_DIM == 0)
    causal = _causal_mask(blk)

    def kv(j0):
        return k_ref[0, pl.ds(j0, blk), :], v_ref[0, pl.ds(j0, blk), :]

    def nearest(q0, has_prev):
        q = q_ref[0, pl.ds(q0, blk), :]
        rest, acc = _sb_block(q, *kv(q0), tri, jnp.zeros((blk, 1), F32), jnp.zeros((blk, HEAD_DIM), F32), causal)
        if has_prev:
            rest, acc = _sb_block(q, *kv(q0 - blk), tri, rest, acc, None)
        return q, rest, acc

    def further(q, j_first, rest, acc):
        def cond(c):
            return jnp.logical_and(c[0] >= 0, c[1] > 0)

        def body(c):
            j, _, rest, acc = c
            rest, acc = _sb_block(q, *kv(pl.multiple_of(j * blk, blk)), tri, rest, acc, None)
            return j - 1, _unfinished(rest), rest, acc

        return lax.while_loop(cond, body, (j_first, _unfinished(rest), rest, acc))[3]

    starts = [i * blk for i in range(s_len // blk)]
    states = [nearest(q0, q0 > 0) for q0 in starts]
    unfinished = functools.reduce(jnp.maximum, [_unfinished(rest) for _, rest, _ in states])
    accs = lax.cond(
        unfinished > 0,
        lambda: tuple(further(q, i - 2, rest, acc) for i, (q, rest, acc) in enumerate(states)),
        lambda: tuple(acc for _, _, acc in states))
    for q0, acc in zip(starts, accs):
        o_ref[0, pl.ds(q0, blk), :] = acc.astype(o_ref.dtype)


def _sb_prompt(q, k, v, blk=256):
    b, s, _ = q.shape
    blk = _tile(s, blk)
    spec = pl.BlockSpec((1, s, HEAD_DIM), lambda bi, h: (bi, 0, h))
    return pl.pallas_call(
        functools.partial(_sb_prompt_kernel, blk=blk),
        grid=(b, HEADS),
        in_specs=[spec, spec, spec],
        out_specs=spec,
        out_shape=jax.ShapeDtypeStruct(q.shape, BF16),
        compiler_params=_params("parallel", "parallel"),
        name="sb_prompt",
    )(q, k, v)


def _sb_sample_kernel(q_ref, kn_ref, vn_ref, kc_ref, vc_ref, o_ref, kbuf, vbuf, sem, rest_ref, acc_ref,
                      *, layer, chunk, blk):
    b = pl.program_id(0)
    n = q_ref.shape[1]
    nc = kc_ref.shape[2] // (chunk * HEADS)
    cols = lambda h: slice(h * HEAD_DIM, (h + 1) * HEAD_DIM)

    def copies(c):
        rows = pl.ds(pl.multiple_of((nc - 1 - c) * (chunk * HEADS), chunk * HEADS), chunk * HEADS)
        return (pltpu.make_async_copy(kc_ref.at[layer, b, rows, :], kbuf, sem.at[0]),
                pltpu.make_async_copy(vc_ref.at[layer, b, rows, :], vbuf, sem.at[1]))

    for cp in copies(0):
        cp.start()

    tri_new = _later_sum_matrix(n, n % HEAD_DIM == 0)
    causal = _causal_mask(n)
    for h in range(HEADS):
        rest, acc = _sb_block(q_ref[0, :, cols(h)], kn_ref[0, :, cols(h)], vn_ref[0, :, cols(h)], tri_new,
                              jnp.zeros((n, 1), F32), jnp.zeros((n, HEAD_DIM), F32), causal)
        rest_ref[h] = rest
        acc_ref[h] = acc

    tri = _later_sum_matrix(blk, blk % HEAD_DIM == 0)

    def chunk_body(carry):
        c, _ = carry
        for cp in copies(c):
            cp.wait()
        live = jnp.int32(0)
        for h in range(HEADS):
            q = q_ref[0, :, cols(h)]
            rest, acc = rest_ref[h], acc_ref[h]
            for jb in reversed(range(chunk // blk)):
                rows = pl.ds(jb * blk * HEADS + h, blk, stride=HEADS)
                rest, acc = _sb_block(q, kbuf[rows, :].astype(BF16), vbuf[rows, :].astype(BF16), tri, rest, acc, None)
            rest_ref[h] = rest
            acc_ref[h] = acc
            live = jnp.maximum(live, _unfinished(rest))
        more = jnp.logical_and(c + 1 < nc, live > 0)

        @pl.when(more)
        def _():
            for cp in copies(c + 1):
                cp.start()

        return c + 1, more.astype(jnp.int32)

    lax.while_loop(lambda carry: carry[1] > 0, chunk_body, (jnp.int32(0), jnp.int32(1)))

    for h in range(HEADS):
        o_ref[0, :, cols(h)] = acc_ref[h].astype(o_ref.dtype)


def _sb_sample(q, k_new, v_new, cache_k, cache_v, layer, chunk=256, blk=256):
    b, n, width = q.shape
    past = cache_k.shape[2] // HEADS
    assert past > 0
    chunk = _tile(past, chunk)
    blk = _tile(chunk, blk)
    new = pl.BlockSpec((1, n, width), lambda bi: (bi, 0, 0))
    hbm = pl.BlockSpec(memory_space=pl.ANY)
    return pl.pallas_call(
        functools.partial(_sb_sample_kernel, layer=layer, chunk=chunk, blk=blk),
        grid=(b,),
        in_specs=[new, new, new, hbm, hbm],
        out_specs=new,
        out_shape=jax.ShapeDtypeStruct(q.shape, BF16),
        scratch_shapes=[pltpu.VMEM((chunk * HEADS, HEAD_DIM), F32), pltpu.VMEM((chunk * HEADS, HEAD_DIM), F32),
                        pltpu.SemaphoreType.DMA((2,)),
                        pltpu.VMEM((HEADS, n, 1), F32), pltpu.VMEM((HEADS, n, HEAD_DIM), F32)],
        compiler_params=_params("parallel"),
        name="sb_sample",
    )(q, k_new, v_new, cache_k, cache_v)


def _mla_advance(state, st, v_prev, mask):
    m, l, acc, p_prev = state
    pv = _tn_dot(v_prev, p_prev)
    if mask is not None:
        st = jnp.where(mask, st, NEG_INF)
    m_new = jnp.maximum(m, jnp.max(st, axis=0, keepdims=True))
    alpha = jnp.exp2(m - m_new)
    p = jnp.exp2(st - m_new)
    l = alpha * l + jnp.sum(p, axis=0, keepdims=True)
    return m_new, l, alpha * (acc + pv), p.astype(BF16)


def _mla_prompt_kernel(q_ref, k_ref, v_ref, o_ref, *, blk):
    s_len = q_ref.shape[1]
    k_idx = lax.broadcasted_iota(jnp.int32, (blk, blk), 0)
    q_idx = lax.broadcasted_iota(jnp.int32, (blk, blk), 1)
    chunk_mask = (k_idx // CHUNK) <= (q_idx // CHUNK)
    init = (jnp.full((1, blk), NEG_INF, F32), jnp.zeros((1, blk), F32), jnp.zeros((HEAD_DIM, blk), F32),
            jnp.zeros((blk, blk), BF16))

    def rows(j):
        return pl.ds(j * blk, blk)

    for i in range(s_len // blk):
        q = q_ref[0, rows(i), :]
        state = init
        for j in range(i + 1):
            st = _nt_dot(k_ref[0, rows(j), :], q)
            state = _mla_advance(state, st, v_ref[0, rows(max(j - 1, 0)), :], chunk_mask if j == i else None)
        _, l, acc, p_last = state
        acc = acc + _tn_dot(v_ref[0, rows(i), :], p_last)
        o_ref[0, rows(i), :] = (acc / l).T.astype(o_ref.dtype)


def _mla_prompt(q, k, v, blk=512):
    b, s, _ = q.shape
    blk = _tile(s, blk)
    assert blk % CHUNK == 0
    qk = pl.BlockSpec((1, s, QK_PAD), lambda bi, h: (bi, 0, h))
    vo = pl.BlockSpec((1, s, HEAD_DIM), lambda bi, h: (bi, 0, h))
    return pl.pallas_call(
        functools.partial(_mla_prompt_kernel, blk=blk),
        grid=(b, HEADS),
        in_specs=[qk, qk, vo],
        out_specs=vo,
        out_shape=jax.ShapeDtypeStruct(v.shape, BF16),
        compiler_params=_params("parallel", "parallel"),
        name="mla_prompt",
    )(q, k, v)


def _mla_sample_kernel(q_ref, cp_ref, rp_ref, cn_ref, rn_ref, wuk_ref, wukt_ref, wuv_ref, g_ref, o_ref,
                       *, past, chunk):
    n = q_ref.shape[1]
    g_nope, g_rope = g_ref[:, :HEAD_DIM], g_ref[:, HEAD_DIM:]
    rows = lambda h: slice(h * n, (h + 1) * n)

    qa, qr = [], []
    for h in range(HEADS):
        c0 = h * QK_PAD
        q_nope = (q_ref[0, :, c0:c0 + HEAD_DIM].astype(F32) * g_nope).astype(BF16)
        qa.append(_nt_dot(q_nope, wuk_ref[:, h * HEAD_DIM:(h + 1) * HEAD_DIM]).astype(BF16))
        q_rope = q_ref[0, :, c0 + HEAD_DIM:c0 + QK_PAD].astype(F32) * g_rope
        qr.append(q_rope[:, :ROPE_DIM].astype(BF16))
    lhs = jnp.concatenate(qa + [wukt_ref[...]], axis=0)
    qr = jnp.concatenate(qr, axis=0)

    def scores(c, kr):
        both = _nt_dot(lhs, c)
        s_rope = _nt_dot(qr, kr.astype(BF16))
        kr_t = kr.T
        ss_kr = jnp.sum(kr_t * kr_t, axis=0, keepdims=True)
        out = []
        for h in range(HEADS):
            kn_t = both[HEADS * n + h * HEAD_DIM:HEADS * n + (h + 1) * HEAD_DIM]
            ss = jnp.sum(kn_t * kn_t, axis=0, keepdims=True) + ss_kr
            rs = lax.rsqrt(ss * (1.0 / QK_DIM) + NORM_EPS)
            out.append((both[rows(h)] + s_rope[rows(h)]) * rs)
        return out

    c_new = cn_ref[...]
    k_pos = past + lax.broadcasted_iota(jnp.int32, (n, n), 1)
    q_pos = past + lax.broadcasted_iota(jnp.int32, (n, n), 0)
    visible = (k_pos // CHUNK) <= (q_pos // CHUNK)
    s = jnp.concatenate([jnp.where(visible, s_h, NEG_INF) for s_h in scores(c_new, rn_ref[...])], axis=0)
    m = jnp.max(s, axis=-1, keepdims=True)
    p = jnp.exp2(s - m)
    l = jnp.sum(p, axis=-1, keepdims=True)
    mix = _dot(p.astype(BF16), c_new)
    for j in range(past // chunk):
        ks = slice(j * chunk, (j + 1) * chunk)
        c = cp_ref[ks, :].astype(BF16)
        s = jnp.concatenate(scores(c, rp_ref[ks, :]), axis=0)
        m_new = jnp.maximum(m, jnp.max(s, axis=-1, keepdims=True))
        alpha = jnp.exp2(m - m_new)
        p = jnp.exp2(s - m_new)
        l = alpha * l + jnp.sum(p, axis=-1, keepdims=True)
        mix = alpha * mix + _dot(p.astype(BF16), c)
        m = m_new
    mix = (mix / l).astype(BF16)
    for h in range(HEADS):
        cols = slice(h * HEAD_DIM, (h + 1) * HEAD_DIM)
        o_ref[0, :, cols] = _dot(mix[rows(h)], wuv_ref[:, cols]).astype(o_ref.dtype)


def _mla_sample(q, cache_c, cache_r, layer, c_new, r_new, r_layer, w, chunk=1024):
    b, n, _ = q.shape
    latent = cache_c.shape[2]
    past = cache_c.shape[1] // b
    chunk = _tile(past, chunk)
    width = HEADS * HEAD_DIM
    const = lambda bi: (0, 0)
    return pl.pallas_call(
        functools.partial(_mla_sample_kernel, past=past, chunk=chunk),
        grid=(b,),
        in_specs=[pl.BlockSpec((1, n, HEADS * QK_PAD), lambda bi: (bi, 0, 0)),
                  pl.BlockSpec((None, past, latent), lambda bi: (layer, bi, 0)),
                  pl.BlockSpec((None, past, ROPE_DIM), lambda bi: (layer, bi, 0)),
                  pl.BlockSpec((n, latent), lambda bi: (bi, 0)),
                  pl.BlockSpec((None, n, ROPE_DIM), lambda bi: (r_layer, bi, 0)),
                  pl.BlockSpec((latent, width), const), pl.BlockSpec((width, latent), const),
                  pl.BlockSpec((latent, width), const), pl.BlockSpec((1, QK_PAD), const)],
        out_specs=pl.BlockSpec((1, n, width), lambda bi: (bi, 0, 0)),
        out_shape=jax.ShapeDtypeStruct((b, n, width), BF16),
        compiler_params=_params("parallel"),
        name="mla_sample",
    )(q, cache_c, cache_r, c_new, r_new, w["uk"], w["uk_t"], w["uv"], w["k_norm_g"])


def _merge_kernel(a1_ref, a2_ref, g_ref, x_ref, w1_ref, w2_ref, wo_ref, o_ref):
    d = o_ref.shape[1]
    y1 = _dot(a1_ref[...], w1_ref[...])
    y2 = _dot(a2_ref[...], w2_ref[...])
    mix = g_ref[:, :d].astype(F32) * y1 + g_ref[:, d:].astype(F32) * y2
    o_ref[...] = x_ref[...] + _dot(mix.astype(BF16), wo_ref[...])


def _merge_residual(sb_out, mla_out, gate, x, w, tm=512):
    m, k = sb_out.shape
    d = x.shape[1]
    tm = _tile(m, tm)
    row = lambda width: pl.BlockSpec((tm, width), lambda i: (i, 0))
    layer = w["layer"]
    resident = lambda shape: pl.BlockSpec((None,) + shape, lambda i: (layer, 0, 0), pipeline_mode=pl.Buffered(1))
    return pl.pallas_call(
        _merge_kernel,
        grid=(m // tm,),
        in_specs=[row(k), row(k), row(2 * d), row(d), resident((k, d)), resident((k, d)), resident((d, d))],
        out_specs=row(d),
        out_shape=jax.ShapeDtypeStruct((m, d), F32),
        compiler_params=_params("parallel"),
        name="merge_residual",
    )(sb_out, mla_out, gate, x, w["sb_proj"], w["mla_proj"], w["o"])


def _ffn_kernel(x_ref, g_ref, wup_ref, wdn_ref, o_ref, h_ref):
    @pl.when(pl.program_id(1) == 0)
    def _():
        x = x_ref[...]
        h_ref[...] = _rms(x, g_ref[...]).astype(h_ref.dtype)
        o_ref[...] = x

    u = _dot(h_ref[...], wup_ref[...])
    u = jnp.square(jnp.maximum(u, 0.0)).astype(BF16)
    o_ref[...] += _dot(u, wdn_ref[...])


def _ffn(x, w, tm=1024, tf=1024):
    m, d = x.shape
    f = w["up"].shape[2]
    layer = w["layer"]
    tm, tf = _tile(m, tm), _tile(f, tf)
    return pl.pallas_call(
        _ffn_kernel,
        grid=(m // tm, f // tf),
        in_specs=[pl.BlockSpec((tm, d), lambda i, j: (i, 0)), pl.BlockSpec((1, d), lambda i, j: (0, 0)),
                  pl.BlockSpec((None, d, tf), lambda i, j: (layer, 0, j)),
                  pl.BlockSpec((None, tf, d), lambda i, j: (layer, j, 0))],
        out_specs=pl.BlockSpec((tm, d), lambda i, j: (i, 0)),
        out_shape=jax.ShapeDtypeStruct((m, d), F32),
        scratch_shapes=[pltpu.VMEM((tm, d), BF16)],
        compiler_params=_params("parallel", "arbitrary", vmem_limit_bytes=FFN_VMEM_LIMIT_BYTES),
        name="ffn",
    )(x, w["norm2_g"], w["up"], w["down"])


def _rope_table(pos):
    inv_freq = ROPE_THETA ** (-jnp.arange(HALF_ROPE, dtype=F32) / HALF_ROPE)
    ang = pos.astype(F32)[:, None] * inv_freq[None, :]
    cos, sin = jnp.cos(ang), jnp.sin(ang)
    return jnp.concatenate([cos, cos, -sin, sin], axis=1)


def _rope_cols(w):
    x1, x2 = w[..., :HALF_ROPE], w[..., HALF_ROPE:]
    return jnp.concatenate([x1, x2, x2, x1], axis=-1)


def _pad_rope_heads(wq):
    col = jnp.arange(HEADS * QK_PAD, dtype=jnp.int32)
    head, lane = col // QK_PAD, col % QK_PAD
    src_lane = jnp.where(lane < QK_DIM, lane, jnp.where(lane < QK_DIM + HALF_ROPE, lane - HALF_ROPE, lane - 3 * HALF_ROPE))
    src = head * QK_DIM + src_lane
    select = (jnp.arange(HEADS * QK_DIM, dtype=jnp.int32)[:, None] == src[None, :]).astype(BF16)
    return jnp.dot(wq, select, preferred_element_type=F32).astype(BF16)


def _pad_gain(g):
    return jnp.concatenate([g, jnp.zeros((QK_PAD - QK_DIM,), g.dtype)]).reshape(1, QK_PAD)


def _layer_weights(l, stacks, norm1_g, w_in, q_norm_g, k_norm_g, kv_norm_g, w_uk, w_uv, norm2_g):
    depth, d = w_in.shape[:2]
    width = HEADS * HEAD_DIM
    latent = w_uk.shape[1]
    wi = w_in[l]
    o_q, o_ckv = 3 * width, 3 * width + HEADS * QK_DIM
    o_kr, o_gate = o_ckv + latent, o_ckv + latent + ROPE_DIM
    bf = lambda a: a.astype(BF16)
    return {
        "depth": depth, "layer": l, **stacks,
        "norm1_g": norm1_g[l], "norm2_g": norm2_g[l].reshape(1, d),
        "sb_qkv": bf(wi[:, :o_q]),
        "mla_q": _pad_rope_heads(bf(wi[:, o_q:o_ckv])),
        "ckv_kr": bf(jnp.concatenate([wi[:, o_ckv:o_kr], _rope_cols(wi[:, o_kr:o_gate])], axis=1)),
        "gate": bf(wi[:, o_gate:]),
        "q_norm_g": _pad_gain(q_norm_g[l]), "k_norm_g": _pad_gain(k_norm_g[l]),
        "kv_norm_g": kv_norm_g[l].reshape(1, latent),
        "uk": bf(w_uk[l]), "uk_t": bf(w_uk[l].T), "uv": bf(w_uv[l]),
    }


def _merge_and_ffn(x, sb_out, mla_out, gate, w):
    return _ffn(_merge_residual(sb_out, mla_out, gate, x, w), w)


def kernel(x_prompt, x_sample, cache_sb_k, cache_sb_v, cache_mla_ckv, cache_mla_krope, norm1_g, w_in, q_norm_g, k_norm_g, kv_norm_g, w_uk, w_uv, w_sb_proj, w_mla_proj, w_o, norm2_g, w_up, w_down):
    b, s, d = x_prompt.shape
    bs, n, _ = x_sample.shape
    depth, _, past = cache_sb_k.shape[:3]
    width = HEADS * HEAD_DIM
    latent = cache_mla_ckv.shape[-1]

    tab_p = jnp.tile(_rope_table(jnp.arange(s, dtype=jnp.int32)), (b, 1))
    tab_s = jnp.tile(_rope_table(past + jnp.arange(n, dtype=jnp.int32)), (bs, 1))
    cache_k = cache_sb_k.reshape(depth, bs, past * HEADS, HEAD_DIM)
    cache_v = cache_sb_v.reshape(depth, bs, past * HEADS, HEAD_DIM)
    cache_c = cache_mla_ckv.reshape(depth, bs * past, latent)
    cache_r = cache_mla_krope.reshape(depth, bs * past, ROPE_DIM)

    xp = x_prompt.reshape(b * s, d)
    xs = x_sample.reshape(bs * n, d)
    new_p = new_s = None
    stacks = {"sb_proj": w_sb_proj.astype(BF16), "mla_proj": w_mla_proj.astype(BF16), "o": w_o.astype(BF16),
              "up": w_up.astype(BF16), "down": w_down.astype(BF16)}
    seq = lambda a: a.reshape(b, s, a.shape[1])
    dec = lambda a: a.reshape(bs, n, a.shape[1])
    for l in range(depth):
        w = _layer_weights(l, stacks, norm1_g, w_in, q_norm_g, k_norm_g, kv_norm_g, w_uk, w_uv, norm2_g)
        q, new_s = _project(xs, tab_s, w, l, new_s)
        sb_out = _sb_sample(dec(q["q"]), dec(q["kb"]), dec(q["vb"]), cache_k, cache_v, l)
        mla_out = _mla_sample(dec(q["qm"]), cache_c, cache_r, l, q["ckvb"], new_s[3], l, w)
        xs = _merge_and_ffn(xs, sb_out.reshape(bs * n, width), mla_out.reshape(bs * n, width), q["gate"], w)
        p, new_p = _project(xp, tab_p, w, l, new_p)
        sb_out = _sb_prompt(seq(p["q"]), seq(p["kb"]), seq(p["vb"]))
        k_mla, v_mla = _kv_project(p["ckvb"][None], 0, new_p[3], l, w)
        mla_out = _mla_prompt(seq(p["qm"]), seq(k_mla), seq(v_mla))
        xp = _merge_and_ffn(xp, sb_out.reshape(b * s, width), mla_out.reshape(b * s, width), p["gate"], w)

    def caches(new, bb, t):
        k32, v32, ckv, kr = new
        return (k32.reshape(depth, bb, t, HEADS, HEAD_DIM), v32.reshape(depth, bb, t, HEADS, HEAD_DIM),
                ckv.reshape(depth, bb, t, latent), kr.reshape(depth, bb, t, ROPE_DIM))

    return (xp.reshape(b, s, d), xs.reshape(bs, n, d), *caches(new_p, b, s), *caches(new_s, bs, n))
```

```python
import functools

import jax
import jax.numpy as jnp
from jax import lax
from jax.experimental import pallas as pl
from jax.experimental.pallas import tpu as pltpu

F32 = jnp.float32
BF16 = jnp.bfloat16

CHUNK = 64
HEADS = 8
HEAD_DIM = 128
ROPE_DIM = 64
HALF_ROPE = ROPE_DIM // 2
QK_DIM = HEAD_DIM + ROPE_DIM
QK_PAD = 2 * HEAD_DIM
ROPE_THETA = 10000.0
NORM_EPS = 1e-6
NEG_INF = -1e30
LOG2E = 1.4426950408889634
SB_SCALE = LOG2E * HEAD_DIM ** -0.5
MLA_SCALE = LOG2E * QK_DIM ** -0.5
SB_DONE = 110.0 * LOG2E

MIB = 1024 * 1024
VMEM_LIMIT_BYTES = 56 * MIB
FFN_VMEM_LIMIT_BYTES = 62 * MIB


def _params(*sem, vmem_limit_bytes=VMEM_LIMIT_BYTES):
    return pltpu.CompilerParams(dimension_semantics=sem, vmem_limit_bytes=vmem_limit_bytes)


def _tile(m, pref):
    t = min(m, pref)
    assert m % t == 0, (m, pref)
    return t


def _nt_dot(a, b):
    return lax.dot_general(a, b, (((1,), (1,)), ((), ())), preferred_element_type=F32)


def _tn_dot(a, b):
    return lax.dot_general(a, b, (((0,), (0,)), ((), ())), preferred_element_type=F32)


def _dot(a, b):
    return jnp.dot(a, b, preferred_element_type=F32)


def _rms(x, g):
    ms = jnp.mean(x * x, axis=-1, keepdims=True)
    return x * lax.rsqrt(ms + NORM_EPS) * g


def _matmul_call(body, a, w, extras, outs, tm, tn, name, carried=(), resident=False):
    m, k = a.shape
    n = w.shape[1]
    tm, tn = _tile(m, tm), _tile(n, tn)
    n_in = 2 + len(extras)
    w_mode = {"pipeline_mode": pl.Buffered(1)} if resident else {}
    in_specs = [pl.BlockSpec((tm, k), lambda i, j: (i, 0)), pl.BlockSpec((k, tn), lambda i, j: (0, j), **w_mode)]
    in_specs += [pl.BlockSpec(bs, im) for _, bs, im in extras]
    in_specs += [pl.BlockSpec(memory_space=pl.ANY)] * len(carried)

    def kernel_fn(*refs):
        body(*refs[:n_in], *refs[n_in + len(carried):])

    return pl.pallas_call(
        kernel_fn,
        grid=(m // tm, n // tn),
        in_specs=in_specs,
        out_specs=[pl.BlockSpec(bs, im) for _, _, bs, im in outs],
        out_shape=[jax.ShapeDtypeStruct(s, d) for s, d, _, _ in outs],
        input_output_aliases={n_in + t: oi for t, (_, oi) in enumerate(carried)},
        compiler_params=_params("parallel", "arbitrary"),
        name=name,
    )(a, w, *[e[0] for e in extras], *[c[0] for c in carried])


def _layer_slab(ref, layer):
    if len(ref.shape) == 2:
        return ref
    for other in range(ref.shape[0]):
        if other != layer:
            ref[other] = jnp.zeros(ref.shape[1:], ref.dtype)
    return ref.at[layer]


def _sb_qkv_kernel(a_ref, w_ref, q_ref, kc_ref, kb_ref, vc_ref, vb_ref, *, layer):
    a = a_ref[...]
    tm = a.shape[0]
    width = q_ref.shape[1]
    q_ref[...] = (_dot(a, w_ref[:, :width]) * SB_SCALE).astype(q_ref.dtype)
    for part, (cache_ref, o_ref) in enumerate(((kc_ref, kb_ref), (vc_ref, vb_ref)), start=1):
        acc = _dot(a, w_ref[:, part * width:(part + 1) * width])
        o_ref[...] = acc.astype(o_ref.dtype)
        slab = _layer_slab(cache_ref, layer)
        for h in range(HEADS):
            slab[pl.ds(h, tm, stride=HEADS), :] = acc[:, h * HEAD_DIM:(h + 1) * HEAD_DIM]


def _norm_gate_kernel(x_ref, g_ref, w_ref, h_ref, o_ref, *, row_parts, col_parts):
    rp = x_ref.shape[0] // row_parts
    cp = w_ref.shape[1] // col_parts
    for r in range(row_parts):
        rows = slice(r * rp, (r + 1) * rp)
        h = _rms(x_ref[rows, :], g_ref[...]).astype(h_ref.dtype)
        h_ref[rows, :] = h
        for c in range(col_parts):
            cols = slice(c * cp, (c + 1) * cp)
            o_ref[rows, cols] = jax.nn.sigmoid(_dot(h, w_ref[:, cols])).astype(o_ref.dtype)


def _norm_gate(x, g, w_gate, tm=512, row_parts=2, col_parts=4):
    m, d = x.shape
    n = w_gate.shape[1]
    tm = _tile(m, tm)
    row = lambda i: (i, 0)
    return pl.pallas_call(
        functools.partial(_norm_gate_kernel, row_parts=row_parts, col_parts=col_parts),
        grid=(m // tm,),
        in_specs=[pl.BlockSpec((tm, d), row), pl.BlockSpec((1, d), lambda i: (0, 0)),
                  pl.BlockSpec((d, n), lambda i: (0, 0), pipeline_mode=pl.Buffered(1))],
        out_specs=[pl.BlockSpec((tm, d), row), pl.BlockSpec((tm, n), row)],
        out_shape=[jax.ShapeDtypeStruct((m, d), BF16), jax.ShapeDtypeStruct((m, n), BF16)],
        compiler_params=_params("parallel"),
        name="norm_gate",
    )(x, g.reshape(1, d), w_gate)


def _rope_tile(x, tab):
    t = x * tab
    return t + pltpu.roll(t, ROPE_DIM, 1)


def _qmla_kernel(a_ref, w_ref, tab_ref, g_ref, o_ref, *, row_parts):
    low = (lax.broadcasted_iota(jnp.int32, (1, HEAD_DIM), 1) < ROPE_DIM).astype(F32)
    g_nope, g_rope = g_ref[:, :HEAD_DIM] * MLA_SCALE, g_ref[:, HEAD_DIM:] * MLA_SCALE
    rp = a_ref.shape[0] // row_parts
    for r in range(row_parts):
        rows = slice(r * rp, (r + 1) * rp)
        a = a_ref[rows, :]
        tab = tab_ref[rows, :]
        for h in range(HEADS):
            c0 = h * QK_PAD
            acc = _dot(a, w_ref[:, c0:c0 + QK_PAD])
            nope = acc[:, :HEAD_DIM]
            rope = _rope_tile(acc[:, HEAD_DIM:], tab) * low
            ss = jnp.sum(nope * nope + rope * rope, axis=-1, keepdims=True)
            rs = lax.rsqrt(ss * (1.0 / QK_DIM) + NORM_EPS)
            o_ref[rows, c0:c0 + HEAD_DIM] = (nope * rs * g_nope).astype(o_ref.dtype)
            o_ref[rows, c0 + HEAD_DIM:c0 + QK_PAD] = (rope * rs * g_rope).astype(o_ref.dtype)


def _ckv_kernel(a_ref, w_ref, tab_ref, g_ref, ckv_ref, kr_ref, ckvb_ref, *, layer):
    a = a_ref[...]
    latent = ckvb_ref.shape[1]
    _layer_slab(kr_ref, layer)[...] = _rope_tile(_dot(a, w_ref[:, latent:]), tab_ref[...])[:, :ROPE_DIM]
    cn = _rms(_dot(a, w_ref[:, :latent]), g_ref[...])
    _layer_slab(ckv_ref, layer)[...] = cn
    ckvb_ref[...] = cn.astype(ckvb_ref.dtype)


def _project(x, tab, w, layer, new_cache, tm=1024):
    m, d = x.shape
    h, gate = _norm_gate(x, w["norm1_g"], w["gate"])
    tm = _tile(m, tm)
    width = HEADS * HEAD_DIM
    depth = w["depth"]
    latent = w["kv_norm_g"].shape[1]
    row = lambda i, j: (i, 0)
    full = lambda shape, dtype: (shape, dtype, (tm, shape[1]), row)
    carry = lambda idx, out: [] if new_cache is None else [(new_cache[idx], out)]

    def slab(rows, width, block_rows):
        if new_cache is None:
            return ((depth, rows, width), F32, (depth, block_rows, width), lambda i, j: (0, i, 0))
        return ((depth, rows, width), F32, (None, block_rows, width), lambda i, j: (layer, i, 0))

    tms = _tile(m, 512)
    cache_out = slab(m * HEADS, HEAD_DIM, tms * HEADS)
    half = ((m, width), BF16, (tms, width), row)
    q, k32, kb, v32, vb = _matmul_call(
        functools.partial(_sb_qkv_kernel, layer=layer), h, w["sb_qkv"], [], [half, cache_out, half, cache_out, half],
        tms, 3 * width, "proj_sb_qkv", carry(0, 1) + carry(1, 3), resident=True)
    tmq = _tile(m, 1024)
    tab_spec = lambda t: (tab, (t, HEAD_DIM), row)
    gq = (w["q_norm_g"], (1, QK_PAD), lambda i, j: (0, 0))
    (qm,) = _matmul_call(functools.partial(_qmla_kernel, row_parts=4), h, w["mla_q"], [tab_spec(tmq), gq],
                         [((m, HEADS * QK_PAD), BF16, (tmq, HEADS * QK_PAD), row)], tmq, HEADS * QK_PAD, "proj_mla_q")
    gkv = (w["kv_norm_g"], (1, latent), lambda i, j: (0, 0))
    ckv32, kr, ckvb = _matmul_call(
        functools.partial(_ckv_kernel, layer=layer), h, w["ckv_kr"], [tab_spec(tm), gkv],
        [slab(m, latent, tm), slab(m, ROPE_DIM, tm), full((m, latent), BF16)],
        tm, latent + HEAD_DIM, "proj_ckv", carry(2, 0) + carry(3, 1))
    return dict(q=q, kb=kb, vb=vb, qm=qm, ckvb=ckvb, gate=gate), (k32, v32, ckv32, kr)


def _kvproj_kernel(c_ref, kr_ref, wuk_ref, wuv_ref, g_ref, k_ref, v_ref):
    c = c_ref[...].astype(BF16)
    kn = _dot(c, wuk_ref[...])
    v_ref[...] = _dot(c, wuv_ref[...]).astype(v_ref.dtype)
    kr = kr_ref[...]
    ss_kr = jnp.sum(kr * kr, axis=-1, keepdims=True)
    kr_pad = jnp.concatenate([kr, jnp.zeros_like(kr)], axis=1)
    g_nope, g_rope = g_ref[:, :HEAD_DIM], g_ref[:, HEAD_DIM:]
    for h in range(HEADS):
        nope = kn[:, h * HEAD_DIM:(h + 1) * HEAD_DIM]
        ss = jnp.sum(nope * nope, axis=-1, keepdims=True) + ss_kr
        rs = lax.rsqrt(ss * (1.0 / QK_DIM) + NORM_EPS)
        c0 = h * QK_PAD
        k_ref[:, c0:c0 + HEAD_DIM] = (nope * rs * g_nope).astype(k_ref.dtype)
        k_ref[:, c0 + HEAD_DIM:c0 + QK_PAD] = (kr_pad * rs * g_rope).astype(k_ref.dtype)


def _kv_project(c, c_layer, kr, kr_layer, w, tm=1024):
    _, rows, latent = c.shape
    tm = _tile(rows, tm)
    width = HEADS * HEAD_DIM
    const = lambda i: (0, 0)
    return pl.pallas_call(
        _kvproj_kernel,
        grid=(rows // tm,),
        in_specs=[pl.BlockSpec((None, tm, latent), lambda i: (c_layer, i, 0)),
                  pl.BlockSpec((None, tm, ROPE_DIM), lambda i: (kr_layer, i, 0)),
                  pl.BlockSpec((latent, width), const), pl.BlockSpec((latent, width), const),
                  pl.BlockSpec((1, QK_PAD), const)],
        out_specs=[pl.BlockSpec((tm, HEADS * QK_PAD), lambda i: (i, 0)), pl.BlockSpec((tm, width), lambda i: (i, 0))],
        out_shape=[jax.ShapeDtypeStruct((rows, HEADS * QK_PAD), BF16), jax.ShapeDtypeStruct((rows, width), BF16)],
        compiler_params=_params("parallel"),
        name="mla_kv_project",
    )(c, kr, w["uk"], w["uv"], w["k_norm_g"])


def _later_sum_matrix(n, doubled):
    rows = 2 * n if doubled else n
    r = lax.broadcasted_iota(jnp.int32, (rows, n), 0)
    c = lax.broadcasted_iota(jnp.int32, (rows, n), 1)
    if doubled:
        r = jnp.where(r >= n, r - n, r)
    return jnp.where(r > c, 1.0, 0.0).astype(BF16)


def _sb_block(q, k, v, tri, rest, acc, mask):
    z = _nt_dot(q, k)
    sp = jnp.maximum(z, 0.0) + jnp.log2(1.0 + jnp.exp2(-jnp.abs(z)))
    if mask is not None:
        sp = jnp.where(mask, sp, 0.0)
    hi = sp.astype(BF16)
    lo = (sp - hi.astype(F32)).astype(BF16)
    if tri.shape[0] == 2 * tri.shape[1]:
        later = _dot(jnp.concatenate([hi, lo], axis=1), tri)
    else:
        later = _dot(hi, tri) + _dot(lo, tri)
    a = jnp.exp2(((z - sp) - later) - rest)
    if mask is not None:
        a = jnp.where(mask, a, 0.0)
    acc = acc + _dot(a.astype(BF16), v)
    rest = rest + jnp.sum(sp, axis=-1, keepdims=True)
    return rest, acc


def _unfinished(rest):
    return (jnp.min(rest) < SB_DONE).astype(jnp.int32)


def _causal_mask(n):
    t_idx = lax.broadcasted_iota(jnp.int32, (n, n), 0)
    s_idx = lax.broadcasted_iota(jnp.int32, (n, n), 1)
    return s_idx < t_idx


def _sb_prompt_kernel(q_ref, k_ref, v_ref, o_ref, *, blk):
    s_len = q_ref.shape[1]
    tri = _later_sum_matrix(blk, blk % HEAD_DIM == 0)
    causal = _causal_mask(blk)

    def kv(j0):
        return k_ref[0, pl.ds(j0, blk), :], v_ref[0, pl.ds(j0, blk), :]

    def nearest(q0, has_prev):
        q = q_ref[0, pl.ds(q0, blk), :]
        rest, acc = _sb_block(q, *kv(q0), tri, jnp.zeros((blk, 1), F32), jnp.zeros((blk, HEAD_DIM), F32), causal)
        if has_prev:
            rest, acc = _sb_block(q, *kv(q0 - blk), tri, rest, acc, None)
        return q, rest, acc

    def further(q, j_first, rest, acc):
        def cond(c):
            return jnp.logical_and(c[0] >= 0, c[1] > 0)

        def body(c):
            j, _, rest, acc = c
            rest, acc = _sb_block(q, *kv(pl.multiple_of(j * blk, blk)), tri, rest, acc, None)
            return j - 1, _unfinished(rest), rest, acc

        return lax.while_loop(cond, body, (j_first, _unfinished(rest), rest, acc))[3]

    starts = [i * blk for i in range(s_len // blk)]
    states = [nearest(q0, q0 > 0) for q0 in starts]
    unfinished = functools.reduce(jnp.maximum, [_unfinished(rest) for _, rest, _ in states])
    accs = lax.cond(
        unfinished > 0,
        lambda: tuple(further(q, i - 2, rest, acc) for i, (q, rest, acc) in enumerate(states)),
        lambda: tuple(acc for _, _, acc in states))
    for q0, acc in zip(starts, accs):
        o_ref[0, pl.ds(q0, blk), :] = acc.astype(o_ref.dtype)


def _sb_prompt(q, k, v, blk=256):
    b, s, _ = q.shape
    blk = _tile(s, blk)
    spec = pl.BlockSpec((1, s, HEAD_DIM), lambda bi, h: (bi, 0, h))
    return pl.pallas_call(
        functools.partial(_sb_prompt_kernel, blk=blk),
        grid=(b, HEADS),
        in_specs=[spec, spec, spec],
        out_specs=spec,
        out_shape=jax.ShapeDtypeStruct(q.shape, BF16),
        compiler_params=_params("parallel", "parallel"),
        name="sb_prompt",
    )(q, k, v)


def _sb_sample_kernel(q_ref, kn_ref, vn_ref, kc_ref, vc_ref, o_ref, kbuf, vbuf, sem, rest_ref, acc_ref,
                      *, layer, chunk, blk):
    b = pl.program_id(0)
    n = q_ref.shape[1]
    nc = kc_ref.shape[2] // (chunk * HEADS)
    cols = lambda h: slice(h * HEAD_DIM, (h + 1) * HEAD_DIM)

    def copies(c):
        rows = pl.ds(pl.multiple_of((nc - 1 - c) * (chunk * HEADS), chunk * HEADS), chunk * HEADS)
        return (pltpu.make_async_copy(kc_ref.at[layer, b, rows, :], kbuf, sem.at[0]),
                pltpu.make_async_copy(vc_ref.at[layer, b, rows, :], vbuf, sem.at[1]))

    for cp in copies(0):
        cp.start()

    tri_new = _later_sum_matrix(n, n % HEAD_DIM == 0)
    causal = _causal_mask(n)
    for h in range(HEADS):
        rest, acc = _sb_block(q_ref[0, :, cols(h)], kn_ref[0, :, cols(h)], vn_ref[0, :, cols(h)], tri_new,
                              jnp.zeros((n, 1), F32), jnp.zeros((n, HEAD_DIM), F32), causal)
        rest_ref[h] = rest
        acc_ref[h] = acc

    tri = _later_sum_matrix(blk, blk % HEAD_DIM == 0)

    def chunk_body(carry):
        c, _ = carry
        for cp in copies(c):
            cp.wait()
        live = jnp.int32(0)
        for h in range(HEADS):
            q = q_ref[0, :, cols(h)]
            rest, acc = rest_ref[h], acc_ref[h]
            for jb in reversed(range(chunk // blk)):
                rows = pl.ds(jb * blk * HEADS + h, blk, stride=HEADS)
                rest, acc = _sb_block(q, kbuf[rows, :].astype(BF16), vbuf[rows, :].astype(BF16), tri, rest, acc, None)
            rest_ref[h] = rest
            acc_ref[h] = acc
            live = jnp.maximum(live, _unfinished(rest))
        more = jnp.logical_and(c + 1 < nc, live > 0)

        @pl.when(more)
        def _():
            for cp in copies(c + 1):
                cp.start()

        return c + 1, more.astype(jnp.int32)

    lax.while_loop(lambda carry: carry[1] > 0, chunk_body, (jnp.int32(0), jnp.int32(1)))

    for h in range(HEADS):
        o_ref[0, :, cols(h)] = acc_ref[h].astype(o_ref.dtype)


def _sb_sample(q, k_new, v_new, cache_k, cache_v, layer, chunk=256, blk=256):
    b, n, width = q.shape
    past = cache_k.shape[2] // HEADS
    assert past > 0
    chunk = _tile(past, chunk)
    blk = _tile(chunk, blk)
    new = pl.BlockSpec((1, n, width), lambda bi: (bi, 0, 0))
    hbm = pl.BlockSpec(memory_space=pl.ANY)
    return pl.pallas_call(
        functools.partial(_sb_sample_kernel, layer=layer, chunk=chunk, blk=blk),
        grid=(b,),
        in_specs=[new, new, new, hbm, hbm],
        out_specs=new,
        out_shape=jax.ShapeDtypeStruct(q.shape, BF16),
        scratch_shapes=[pltpu.VMEM((chunk * HEADS, HEAD_DIM), F32), pltpu.VMEM((chunk * HEADS, HEAD_DIM), F32),
                        pltpu.SemaphoreType.DMA((2,)),
                        pltpu.VMEM((HEADS, n, 1), F32), pltpu.VMEM((HEADS, n, HEAD_DIM), F32)],
        compiler_params=_params("parallel"),
        name="sb_sample",
    )(q, k_new, v_new, cache_k, cache_v)


def _mla_advance(state, st, v_prev, mask):
    m, l, acc, p_prev = state
    pv = _tn_dot(v_prev, p_prev)
    if mask is not None:
        st = jnp.where(mask, st, NEG_INF)
    m_new = jnp.maximum(m, jnp.max(st, axis=0, keepdims=True))
    alpha = jnp.exp2(m - m_new)
    p = jnp.exp2(st - m_new)
    l = alpha * l + jnp.sum(p, axis=0, keepdims=True)
    return m_new, l, alpha * (acc + pv), p.astype(BF16)


def _mla_prompt_kernel(q_ref, k_ref, v_ref, o_ref, *, blk):
    s_len = q_ref.shape[1]
    k_idx = lax.broadcasted_iota(jnp.int32, (blk, blk), 0)
    q_idx = lax.broadcasted_iota(jnp.int32, (blk, blk), 1)
    chunk_mask = (k_idx // CHUNK) <= (q_idx // CHUNK)
    init = (jnp.full((1, blk), NEG_INF, F32), jnp.zeros((1, blk), F32), jnp.zeros((HEAD_DIM, blk), F32),
            jnp.zeros((blk, blk), BF16))

    def rows(j):
        return pl.ds(j * blk, blk)

    for i in range(s_len // blk):
        q = q_ref[0, rows(i), :]
        state = init
        for j in range(i + 1):
            st = _nt_dot(k_ref[0, rows(j), :], q)
            state = _mla_advance(state, st, v_ref[0, rows(max(j - 1, 0)), :], chunk_mask if j == i else None)
        _, l, acc, p_last = state
        acc = acc + _tn_dot(v_ref[0, rows(i), :], p_last)
        o_ref[0, rows(i), :] = (acc / l).T.astype(o_ref.dtype)


def _mla_prompt(q, k, v, blk=512):
    b, s, _ = q.shape
    blk = _tile(s, blk)
    assert blk % CHUNK == 0
    qk = pl.BlockSpec((1, s, QK_PAD), lambda bi, h: (bi, 0, h))
    vo = pl.BlockSpec((1, s, HEAD_DIM), lambda bi, h: (bi, 0, h))
    return pl.pallas_call(
        functools.partial(_mla_prompt_kernel, blk=blk),
        grid=(b, HEADS),
        in_specs=[qk, qk, vo],
        out_specs=vo,
        out_shape=jax.ShapeDtypeStruct(v.shape, BF16),
        compiler_params=_params("parallel", "parallel"),
        name="mla_prompt",
    )(q, k, v)


def _mla_sample_kernel(q_ref, cp_ref, rp_ref, cn_ref, rn_ref, wuk_ref, wukt_ref, wuv_ref, g_ref, o_ref,
                       *, past, chunk):
    n = q_ref.shape[1]
    g_nope, g_rope = g_ref[:, :HEAD_DIM], g_ref[:, HEAD_DIM:]
    rows = lambda h: slice(h * n, (h + 1) * n)

    qa, qr = [], []
    for h in range(HEADS):
        c0 = h * QK_PAD
        q_nope = (q_ref[0, :, c0:c0 + HEAD_DIM].astype(F32) * g_nope).astype(BF16)
        qa.append(_nt_dot(q_nope, wuk_ref[:, h * HEAD_DIM:(h + 1) * HEAD_DIM]).astype(BF16))
        q_rope = q_ref[0, :, c0 + HEAD_DIM:c0 + QK_PAD].astype(F32) * g_rope
        qr.append(q_rope[:, :ROPE_DIM].astype(BF16))
    lhs = jnp.concatenate(qa + [wukt_ref[...]], axis=0)
    qr = jnp.concatenate(qr, axis=0)

    def scores(c, kr):
        both = _nt_dot(lhs, c)
        s_rope = _nt_dot(qr, kr.astype(BF16))
        kr_t = kr.T
        ss_kr = jnp.sum(kr_t * kr_t, axis=0, keepdims=True)
        out = []
        for h in range(HEADS):
            kn_t = both[HEADS * n + h * HEAD_DIM:HEADS * n + (h + 1) * HEAD_DIM]
            ss = jnp.sum(kn_t * kn_t, axis=0, keepdims=True) + ss_kr
            rs = lax.rsqrt(ss * (1.0 / QK_DIM) + NORM_EPS)
            out.append((both[rows(h)] + s_rope[rows(h)]) * rs)
        return out

    c_new = cn_ref[...]
    k_pos = past + lax.broadcasted_iota(jnp.int32, (n, n), 1)
    q_pos = past + lax.broadcasted_iota(jnp.int32, (n, n), 0)
    visible = (k_pos // CHUNK) <= (q_pos // CHUNK)
    s = jnp.concatenate([jnp.where(visible, s_h, NEG_INF) for s_h in scores(c_new, rn_ref[...])], axis=0)
    m = jnp.max(s, axis=-1, keepdims=True)
    p = jnp.exp2(s - m)
    l = jnp.sum(p, axis=-1, keepdims=True)
    mix = _dot(p.astype(BF16), c_new)
    for j in range(past // chunk):
        ks = slice(j * chunk, (j + 1) * chunk)
        c = cp_ref[ks, :].astype(BF16)
        s = jnp.concatenate(scores(c, rp_ref[ks, :]), axis=0)
        m_new = jnp.maximum(m, jnp.max(s, axis=-1, keepdims=True))
        alpha = jnp.exp2(m - m_new)
        p = jnp.exp2(s - m_new)
        l = alpha * l + jnp.sum(p, axis=-1, keepdims=True)
        mix = alpha * mix + _dot(p.astype(BF16), c)
        m = m_new
    mix = (mix / l).astype(BF16)
    for h in range(HEADS):
        cols = slice(h * HEAD_DIM, (h + 1) * HEAD_DIM)
        o_ref[0, :, cols] = _dot(mix[rows(h)], wuv_ref[:, cols]).astype(o_ref.dtype)


def _mla_sample(q, cache_c, cache_r, layer, c_new, r_new, r_layer, w, chunk=1024):
    b, n, _ = q.shape
    latent = cache_c.shape[2]
    past = cache_c.shape[1] // b
    chunk = _tile(past, chunk)
    width = HEADS * HEAD_DIM
    const = lambda bi: (0, 0)
    return pl.pallas_call(
        functools.partial(_mla_sample_kernel, past=past, chunk=chunk),
        grid=(b,),
        in_specs=[pl.BlockSpec((1, n, HEADS * QK_PAD), lambda bi: (bi, 0, 0)),
                  pl.BlockSpec((None, past, latent), lambda bi: (layer, bi, 0)),
                  pl.BlockSpec((None, past, ROPE_DIM), lambda bi: (layer, bi, 0)),
                  pl.BlockSpec((n, latent), lambda bi: (bi, 0)),
                  pl.BlockSpec((None, n, ROPE_DIM), lambda bi: (r_layer, bi, 0)),
                  pl.BlockSpec((latent, width), const), pl.BlockSpec((width, latent), const),
                  pl.BlockSpec((latent, width), const), pl.BlockSpec((1, QK_PAD), const)],
        out_specs=pl.BlockSpec((1, n, width), lambda bi: (bi, 0, 0)),
        out_shape=jax.ShapeDtypeStruct((b, n, width), BF16),
        compiler_params=_params("parallel"),
        name="mla_sample",
    )(q, cache_c, cache_r, c_new, r_new, w["uk"], w["uk_t"], w["uv"], w["k_norm_g"])


def _merge_kernel(a1_ref, a2_ref, g_ref, x_ref, w1_ref, w2_ref, wo_ref, o_ref):
    d = o_ref.shape[1]
    y1 = _dot(a1_ref[...], w1_ref[...])
    y2 = _dot(a2_ref[...], w2_ref[...])
    mix = g_ref[:, :d].astype(F32) * y1 + g_ref[:, d:].astype(F32) * y2
    o_ref[...] = x_ref[...] + _dot(mix.astype(BF16), wo_ref[...])


def _merge_residual(sb_out, mla_out, gate, x, w, tm=512):
    m, k = sb_out.shape
    d = x.shape[1]
    tm = _tile(m, tm)
    row = lambda width: pl.BlockSpec((tm, width), lambda i: (i, 0))
    layer = w["layer"]
    resident = lambda shape: pl.BlockSpec((None,) + shape, lambda i: (layer, 0, 0), pipeline_mode=pl.Buffered(1))
    return pl.pallas_call(
        _merge_kernel,
        grid=(m // tm,),
        in_specs=[row(k), row(k), row(2 * d), row(d), resident((k, d)), resident((k, d)), resident((d, d))],
        out_specs=row(d),
        out_shape=jax.ShapeDtypeStruct((m, d), F32),
        compiler_params=_params("parallel"),
        name="merge_residual",
    )(sb_out, mla_out, gate, x, w["sb_proj"], w["mla_proj"], w["o"])


def _ffn_kernel(x_ref, g_ref, wup_ref, wdn_ref, o_ref, h_ref, *, row_parts):
    def tile(h):
        u = _dot(h, wup_ref[...])
        return _dot(jnp.square(jnp.maximum(u, 0.0)).astype(BF16), wdn_ref[...])

    @pl.when(pl.program_id(1) == 0)
    def _():
        rp = x_ref.shape[0] // row_parts
        for r in range(row_parts):
            rows = slice(r * rp, (r + 1) * rp)
            x = x_ref[rows, :]
            h = _rms(x, g_ref[...]).astype(h_ref.dtype)
            h_ref[rows, :] = h
            o_ref[rows, :] = x + tile(h)

    @pl.when(pl.program_id(1) > 0)
    def _():
        o_ref[...] += tile(h_ref[...])


def _ffn(x, w, tm=1024, tf=1024, row_parts=4):
    m, d = x.shape
    f = w["up"].shape[2]
    layer = w["layer"]
    tm, tf = _tile(m, tm), _tile(f, tf)
    return pl.pallas_call(
        functools.partial(_ffn_kernel, row_parts=row_parts),
        grid=(m // tm, f // tf),
        in_specs=[pl.BlockSpec((tm, d), lambda i, j: (i, 0)), pl.BlockSpec((1, d), lambda i, j: (0, 0)),
                  pl.BlockSpec((None, d, tf), lambda i, j: (layer, 0, j)),
                  pl.BlockSpec((None, tf, d), lambda i, j: (layer, j, 0))],
        out_specs=pl.BlockSpec((tm, d), lambda i, j: (i, 0)),
        out_shape=jax.ShapeDtypeStruct((m, d), F32),
        scratch_shapes=[pltpu.VMEM((tm, d), BF16)],
        compiler_params=_params("parallel", "arbitrary", vmem_limit_bytes=FFN_VMEM_LIMIT_BYTES),
        name="ffn",
    )(x, w["norm2_g"], w["up"], w["down"])


def _rope_table(pos):
    inv_freq = ROPE_THETA ** (-jnp.arange(HALF_ROPE, dtype=F32) / HALF_ROPE)
    ang = pos.astype(F32)[:, None] * inv_freq[None, :]
    cos, sin = jnp.cos(ang), jnp.sin(ang)
    return jnp.concatenate([cos, cos, -sin, sin], axis=1)


def _rope_cols(w):
    x1, x2 = w[..., :HALF_ROPE], w[..., HALF_ROPE:]
    return jnp.concatenate([x1, x2, x2, x1], axis=-1)


def _pad_rope_heads(wq):
    col = jnp.arange(HEADS * QK_PAD, dtype=jnp.int32)
    head, lane = col // QK_PAD, col % QK_PAD
    src_lane = jnp.where(lane < QK_DIM, lane, jnp.where(lane < QK_DIM + HALF_ROPE, lane - HALF_ROPE, lane - 3 * HALF_ROPE))
    src = head * QK_DIM + src_lane
    select = (jnp.arange(HEADS * QK_DIM, dtype=jnp.int32)[:, None] == src[None, :]).astype(BF16)
    return jnp.dot(wq, select, preferred_element_type=F32).astype(BF16)


def _pad_gain(g):
    return jnp.concatenate([g, jnp.zeros((QK_PAD - QK_DIM,), g.dtype)]).reshape(1, QK_PAD)


def _layer_weights(l, stacks, norm1_g, w_in, q_norm_g, k_norm_g, kv_norm_g, w_uk, w_uv, norm2_g):
    depth, d = w_in.shape[:2]
    width = HEADS * HEAD_DIM
    latent = w_uk.shape[1]
    wi = w_in[l]
    o_q, o_ckv = 3 * width, 3 * width + HEADS * QK_DIM
    o_kr, o_gate = o_ckv + latent, o_ckv + latent + ROPE_DIM
    bf = lambda a: a.astype(BF16)
    return {
        "depth": depth, "layer": l, **stacks,
        "norm1_g": norm1_g[l], "norm2_g": norm2_g[l].reshape(1, d),
        "sb_qkv": bf(wi[:, :o_q]),
        "mla_q": _pad_rope_heads(bf(wi[:, o_q:o_ckv])),
        "ckv_kr": bf(jnp.concatenate([wi[:, o_ckv:o_kr], _rope_cols(wi[:, o_kr:o_gate])], axis=1)),
        "gate": bf(wi[:, o_gate:]),
        "q_norm_g": _pad_gain(q_norm_g[l]), "k_norm_g": _pad_gain(k_norm_g[l]),
        "kv_norm_g": kv_norm_g[l].reshape(1, latent),
        "uk": bf(w_uk[l]), "uk_t": bf(w_uk[l].T), "uv": bf(w_uv[l]),
    }


def _merge_and_ffn(x, sb_out, mla_out, gate, w):
    return _ffn(_merge_residual(sb_out, mla_out, gate, x, w), w)


def kernel(x_prompt, x_sample, cache_sb_k, cache_sb_v, cache_mla_ckv, cache_mla_krope, norm1_g, w_in, q_norm_g, k_norm_g, kv_norm_g, w_uk, w_uv, w_sb_proj, w_mla_proj, w_o, norm2_g, w_up, w_down):
    b, s, d = x_prompt.shape
    bs, n, _ = x_sample.shape
    depth, _, past = cache_sb_k.shape[:3]
    width = HEADS * HEAD_DIM
    latent = cache_mla_ckv.shape[-1]

    tab_p = jnp.tile(_rope_table(jnp.arange(s, dtype=jnp.int32)), (b, 1))
    tab_s = jnp.tile(_rope_table(past + jnp.arange(n, dtype=jnp.int32)), (bs, 1))
    cache_k = cache_sb_k.reshape(depth, bs, past * HEADS, HEAD_DIM)
    cache_v = cache_sb_v.reshape(depth, bs, past * HEADS, HEAD_DIM)
    cache_c = cache_mla_ckv.reshape(depth, bs * past, latent)
    cache_r = cache_mla_krope.reshape(depth, bs * past, ROPE_DIM)

    xp = x_prompt.reshape(b * s, d)
    xs = x_sample.reshape(bs * n, d)
    new_p = new_s = None
    stacks = {"sb_proj": w_sb_proj.astype(BF16), "mla_proj": w_mla_proj.astype(BF16), "o": w_o.astype(BF16),
              "up": w_up.astype(BF16), "down": w_down.astype(BF16)}
    seq = lambda a: a.reshape(b, s, a.shape[1])
    dec = lambda a: a.reshape(bs, n, a.shape[1])
    for l in range(depth):
        w = _layer_weights(l, stacks, norm1_g, w_in, q_norm_g, k_norm_g, kv_norm_g, w_uk, w_uv, norm2_g)
        q, new_s = _project(xs, tab_s, w, l, new_s)
        sb_out = _sb_sample(dec(q["q"]), dec(q["kb"]), dec(q["vb"]), cache_k, cache_v, l)
        mla_out = _mla_sample(dec(q["qm"]), cache_c, cache_r, l, q["ckvb"], new_s[3], l, w)
        xs = _merge_and_ffn(xs, sb_out.reshape(bs * n, width), mla_out.reshape(bs * n, width), q["gate"], w)
        p, new_p = _project(xp, tab_p, w, l, new_p)
        sb_out = _sb_prompt(seq(p["q"]), seq(p["kb"]), seq(p["vb"]))
        k_mla, v_mla = _kv_project(p["ckvb"][None], 0, new_p[3], l, w)
        mla_out = _mla_prompt(seq(p["qm"]), seq(k_mla), seq(v_mla))
        xp = _merge_and_ffn(xp, sb_out.reshape(b * s, width), mla_out.reshape(b * s, width), p["gate"], w)

    def caches(new, bb, t):
        k32, v32, ckv, kr = new
        return (k32.reshape(depth, bb, t, HEADS, HEAD_DIM), v32.reshape(depth, bb, t, HEADS, HEAD_DIM),
                ckv.reshape(depth, bb, t, latent), kr.reshape(depth, bb, t, ROPE_DIM))

    return (xp.reshape(b, s, d), xs.reshape(bs, n, d), *caches(new_p, b, s), *caches(new_s, bs, n))
```
